```python
import jax, jax.numpy as jnp
from jax import lax
import numpy as np

D_MODEL = 1024
BATCH = 2
SEQ = 8192
DEPTH = 2

D_FF = 2816
CONV_WIDTH = 512
CONV_K = 31
SGU_WIDTH = 512
SGU_GROUPS = 4
SGU_CHUNK = 128
ATT_HEADS = 8
HEAD_DIM = 64
ATT_WIDTH = ATT_HEADS * HEAD_DIM
MOBA_BLOCK = 256
MOBA_TOPK = 3
Q_CHUNK = 64
ROPE_THETA = 500000.0
ROPE_DIM = HEAD_DIM // 4
N_BRANCH = 3
EPS = 1e-6
N_IN = 2 * CONV_WIDTH + 2 * SGU_WIDTH + 3 * ATT_WIDTH + N_BRANCH * D_MODEL

kernel_name = "hybrid_gated_conv_gmlp_moba_macaron"


def _rmsnorm(x, g):
    xf = x.astype(jnp.float32)
    y = xf * lax.rsqrt(jnp.mean(xf * xf, axis=-1, keepdims=True) + EPS)
    return (y * g.astype(jnp.float32)).astype(x.dtype)


def _layernorm(x, g, b):
    xf = x.astype(jnp.float32)
    mu = jnp.mean(xf, axis=-1, keepdims=True)
    var = jnp.mean(jnp.square(xf - mu), axis=-1, keepdims=True)
    y = (xf - mu) * lax.rsqrt(var + EPS)
    return (y * g.astype(jnp.float32) + b.astype(jnp.float32)).astype(x.dtype)


def _swiglu(x, wi, wo):
    gate, up = jnp.split(x @ wi, 2, axis=-1)
    return (jax.nn.silu(gate) * up) @ wo


def _rope_tables(seq):
    pos = jnp.arange(seq, dtype=jnp.float32)
    inv_freq = ROPE_THETA ** (-jnp.arange(0, ROPE_DIM, 2, dtype=jnp.float32) / ROPE_DIM)
    ang = pos[:, None] * inv_freq[None, :]
    return jnp.cos(ang), jnp.sin(ang)


def _partial_rope(x, cos, sin):
    c = cos[None, :, None, :].astype(x.dtype)
    s = sin[None, :, None, :].astype(x.dtype)
    half = ROPE_DIM // 2
    x1 = x[..., :half]
    x2 = x[..., half:ROPE_DIM]
    return jnp.concatenate([x1 * c - x2 * s, x2 * c + x1 * s, x[..., ROPE_DIM:]], axis=-1)


def _conv_module(z, conv_w, conv_b, ln_g, ln_b):
    a, g = jnp.split(z, 2, axis=-1)
    y = a * jax.nn.sigmoid(g)
    y = lax.conv_general_dilated(
        y, conv_w[:, None, :], window_strides=(1,),
        padding=[(CONV_K - 1, 0)],
        dimension_numbers=("NWC", "WIO", "NWC"),
        feature_group_count=CONV_WIDTH) + conv_b
    y = _layernorm(y, ln_g, ln_b)
    return jax.nn.silu(y)


def _sgu_module(z, ln_g, ln_b, w_s, b_s):
    u, v = jnp.split(jax.nn.gelu(z), 2, axis=-1)
    v = _layernorm(v, ln_g, ln_b)
    bsz, seq, _ = v.shape
    vc = v.reshape(bsz, seq // SGU_CHUNK, SGU_CHUNK, SGU_GROUPS, SGU_WIDTH // SGU_GROUPS)
    causal = jnp.tril(jnp.ones((SGU_CHUNK, SGU_CHUNK), dtype=bool))
    w = jnp.where(causal[None], w_s, 0.0)
    mixed = jnp.einsum("gts,bnsgc->bntgc", w, vc) + b_s.T[None, None, :, :, None]
    return u * mixed.reshape(bsz, seq, SGU_WIDTH)


def _moba_attention(q, k, v):
    bsz, seq, nh, dh = q.shape
    s_pad = -(-seq // MOBA_BLOCK) * MOBA_BLOCK
    pad = [(0, 0), (0, s_pad - seq), (0, 0), (0, 0)]
    qh = jnp.pad(q, pad).transpose(0, 2, 1, 3)
    kh = jnp.pad(k, pad).transpose(0, 2, 1, 3)
    vh = jnp.pad(v, pad).transpose(0, 2, 1, 3)
    nb = s_pad // MOBA_BLOCK
    kb = kh.reshape(bsz, nh, nb, MOBA_BLOCK, dh)
    vb = vh.reshape(bsz, nh, nb, MOBA_BLOCK, dh)

    kmean = jnp.mean(kb.astype(jnp.float32), axis=3).astype(q.dtype)
    gate = jnp.einsum("bhsd,bhnd->bhsn", qh, kmean).astype(jnp.float32)
    qblk = jnp.arange(s_pad) // MOBA_BLOCK
    past = jnp.arange(nb)[None, :] < qblk[:, None]
    gate = jnp.where(past[None, None], gate, -jnp.inf)
    k_top = min(MOBA_TOPK, nb)
    _, sel = lax.top_k(gate, k_top)
    sel_valid = sel < qblk[None, None, :, None]
    own = jnp.broadcast_to(qblk[None, None, :, None], (bsz, nh, s_pad, 1))
    idx = jnp.concatenate([sel, own.astype(sel.dtype)], axis=-1)
    n_sel = k_top + 1

    n_chunks = s_pad // Q_CHUNK
    q_c = qh.reshape(bsz, nh, n_chunks, Q_CHUNK, dh).transpose(2, 0, 1, 3, 4)
    idx_c = idx.reshape(bsz, nh, n_chunks, Q_CHUNK, n_sel).transpose(2, 0, 1, 3, 4)
    val_c = sel_valid.reshape(bsz, nh, n_chunks, Q_CHUNK, k_top).transpose(2, 0, 1, 3, 4)
    bi = jnp.arange(bsz)[:, None, None, None]
    hi = jnp.arange(nh)[None, :, None, None]
    key_off = jnp.arange(MOBA_BLOCK)
    scale = HEAD_DIM ** -0.5

    def step(args):
        qc, ic, vc, c = args
        kg = kb[bi, hi, ic]
        vg = vb[bi, hi, ic]
        s = jnp.einsum("bhqd,bhqnkd->bhqnk", qc, kg).astype(jnp.float32) * scale
        qpos = c * Q_CHUNK + jnp.arange(Q_CHUNK)
        own_kpos = ic[..., -1:, None] * MOBA_BLOCK + key_off
        own_mask = own_kpos <= qpos[None, None, :, None, None]
        sel_mask = jnp.broadcast_to(vc[..., None], (bsz, nh, Q_CHUNK, k_top, MOBA_BLOCK))
        mask = jnp.concatenate([sel_mask, own_mask], axis=3)
        s = jnp.where(mask, s, -jnp.inf)
        p = jax.nn.softmax(s.reshape(bsz, nh, Q_CHUNK, n_sel * MOBA_BLOCK), axis=-1)
        p = p.reshape(bsz, nh, Q_CHUNK, n_sel, MOBA_BLOCK).astype(vg.dtype)
        return jnp.einsum("bhqnk,bhqnkd->bhqd", p, vg)

    out = lax.map(step, (q_c, idx_c, val_c, jnp.arange(n_chunks)))
    out = out.transpose(1, 0, 3, 2, 4).reshape(bsz, s_pad, nh * dh)
    return out[:, :seq]


def _layer(x, cos, sin, ffn1_norm, ffn1_wi, ffn1_wo, mix_norm, w_in, conv_w, conv_b,
           conv_ln_g, conv_ln_b, sgu_ln_g, sgu_ln_b, sgu_w, sgu_b, w_branch, gate_b,
           w_out, ffn2_norm, ffn2_wi, ffn2_wo):
    bsz, seq, _ = x.shape
    x = x + 0.5 * _swiglu(_rmsnorm(x, ffn1_norm), ffn1_wi, ffn1_wo)

    h = _rmsnorm(x, mix_norm)
    z = h @ w_in
    splits = list(np.cumsum([2 * CONV_WIDTH, 2 * SGU_WIDTH, ATT_WIDTH, ATT_WIDTH, ATT_WIDTH]))
    z_conv, z_sgu, z_q, z_k, z_v, z_gate = jnp.split(z, splits, axis=-1)

    y_a = _conv_module(z_conv, conv_w, conv_b, conv_ln_g, conv_ln_b)
    y_b = _sgu_module(z_sgu, sgu_ln_g, sgu_ln_b, sgu_w, sgu_b)
    q = _partial_rope(z_q.reshape(bsz, seq, ATT_HEADS, HEAD_DIM), cos, sin)
    k = _partial_rope(z_k.reshape(bsz, seq, ATT_HEADS, HEAD_DIM), cos, sin)
    v = z_v.reshape(bsz, seq, ATT_HEADS, HEAD_DIM)
    y_c = _moba_attention(q, k, v)

    branches = jnp.stack([y_a, y_b, y_c], axis=2)
    proj = jnp.einsum("bsnc,ncd->bsnd", branches, w_branch)
    gates = jax.nn.sigmoid(z_gate.reshape(bsz, seq, N_BRANCH, D_MODEL) + gate_b)
    merged = jnp.sum(gates * proj, axis=2)
    x = x + merged @ w_out

    x = x + 0.5 * _swiglu(_rmsnorm(x, ffn2_norm), ffn2_wi, ffn2_wo)
    return x


def setup_inputs(seed: int = 0) -> dict:
    key = jax.random.key(seed)
    ks = jax.random.split(key, 24)
    L, D = DEPTH, D_MODEL

    def nrm(k, shape, scale):
        return jax.random.normal(k, shape, jnp.float32) * scale

    return {
        "x": nrm(ks[0], (BATCH, SEQ, D), 1.0),
        "ffn1_norm": 1.0 + nrm(ks[1], (L, D), 0.01),
        "ffn1_wi": nrm(ks[2], (L, D, 2 * D_FF), D ** -0.5),
        "ffn1_wo": nrm(ks[3], (L, D_FF, D), D_FF ** -0.5),
        "mix_norm": 1.0 + nrm(ks[4], (L, D), 0.01),
        "w_in": nrm(ks[5], (L, D, N_IN), D ** -0.5),
        "conv_w": nrm(ks[6], (L, CONV_K, CONV_WIDTH), CONV_K ** -0.5),
        "conv_b": nrm(ks[7], (L, CONV_WIDTH), 0.01),
        "conv_ln_g": 1.0 + nrm(ks[8], (L, CONV_WIDTH), 0.01),
        "conv_ln_b": nrm(ks[9], (L, CONV_WIDTH), 0.01),
        "sgu_ln_g": 1.0 + nrm(ks[10], (L, SGU_WIDTH), 0.01),
        "sgu_ln_b": nrm(ks[11], (L, SGU_WIDTH), 0.01),
        "sgu_w": nrm(ks[12], (L, SGU_GROUPS, SGU_CHUNK, SGU_CHUNK), SGU_CHUNK ** -0.5),
        "sgu_b": 1.0 + nrm(ks[13], (L, SGU_GROUPS, SGU_CHUNK), 0.01),
        "w_branch": nrm(ks[14], (L, N_BRANCH, CONV_WIDTH, D), CONV_WIDTH ** -0.5),
        "gate_b": nrm(ks[15], (L, N_BRANCH, D), 0.01),
        "w_out": nrm(ks[16], (L, D, D), D ** -0.5),
        "ffn2_norm": 1.0 + nrm(ks[17], (L, D), 0.01),
        "ffn2_wi": nrm(ks[18], (L, D, 2 * D_FF), D ** -0.5),
        "ffn2_wo": nrm(ks[19], (L, D_FF, D), D_FF ** -0.5),
        "final_norm": 1.0 + nrm(ks[20], (D,), 0.01),
    }


def reference(x, ffn1_norm, ffn1_wi, ffn1_wo, mix_norm, w_in, conv_w, conv_b, conv_ln_g,
              conv_ln_b, sgu_ln_g, sgu_ln_b, sgu_w, sgu_b, w_branch, gate_b, w_out,
              ffn2_norm, ffn2_wi, ffn2_wo, final_norm):
    cos, sin = _rope_tables(x.shape[1])
    for l in range(DEPTH):
        x = _layer(x, cos, sin, ffn1_norm[l], ffn1_wi[l], ffn1_wo[l], mix_norm[l], w_in[l],
                   conv_w[l], conv_b[l], conv_ln_g[l], conv_ln_b[l], sgu_ln_g[l],
                   sgu_ln_b[l], sgu_w[l], sgu_b[l], w_branch[l], gate_b[l], w_out[l],
                   ffn2_norm[l], ffn2_wi[l], ffn2_wo[l])
    return _rmsnorm(x, final_norm)
```

```python
import functools

import numpy as np
import jax
import jax.numpy as jnp
from jax import lax
from jax.experimental import pallas as pl
from jax.experimental.pallas import tpu as pltpu

D_FF = 2816
CONV_WIDTH = 512
CONV_K = 31
SGU_WIDTH = 512
SGU_GROUPS = 4
SGU_CHUNK = 128
ATT_HEADS = 8
HEAD_DIM = 64
ATT_WIDTH = ATT_HEADS * HEAD_DIM
MOBA_BLOCK = 256
MOBA_TOPK = 3
ROPE_THETA = 500000.0
ROPE_DIM = HEAD_DIM // 4
N_BRANCH = 3
EPS = 1e-6

LANES = 128
SUBLANES = 8
VMEM_LIMIT = 56 * 1024 * 1024
MASK_VALUE = -1e30
CONV_HALO = 32
CONV_ROWS = 32

BF16 = jnp.bfloat16
F32 = jnp.float32


def _dot(a, b):
    return jnp.dot(a, b, preferred_element_type=F32)


def _sigmoid(x):
    return 1.0 / (1.0 + jnp.exp(-x))


def _rms(x, g):
    return x * lax.rsqrt(jnp.mean(x * x, axis=-1, keepdims=True) + EPS) * g


def _params(sem):
    return pltpu.CompilerParams(dimension_semantics=sem, vmem_limit_bytes=VMEM_LIMIT)


def _ffn_body(x_ref, g_ref, wg_ref, wu_ref, wo_ref, g2_ref, *refs, post):
    if post == "both":
        out_ref, nxt_ref, xn_ref, acc_ref = refs
    else:
        out_ref, xn_ref, acc_ref = refs
    j = pl.program_id(1)

    @pl.when(j == 0)
    def _():
        xn_ref[...] = _rms(x_ref[...], g_ref[...]).astype(BF16)

    xn = xn_ref[...]
    gate = _dot(xn, wg_ref[...])
    up = _dot(xn, wu_ref[...])
    act = (gate * _sigmoid(gate) * up).astype(BF16)
    part = _dot(act, wo_ref[...])

    @pl.when(j == 0)
    def _():
        acc_ref[...] = part

    @pl.when(j > 0)
    def _():
        acc_ref[...] += part

    @pl.when(j == pl.num_programs(1) - 1)
    def _():
        y = x_ref[...] + 0.5 * acc_ref[...]
        if post == "both":
            out_ref[...] = y
            nxt_ref[...] = _rms(y, g2_ref[...]).astype(nxt_ref.dtype)
        elif post == "norm_only":
            out_ref[...] = _rms(y, g2_ref[...])
        else:
            out_ref[...] = y


def _ffn(x, g, wi, wo, g2, *, post, tm=512, tf=1408):
    t, d = x.shape
    nf = D_FF // tf
    row = pl.BlockSpec((tm, d), lambda i, j: (i, 0))
    vec = pl.BlockSpec((1, d), lambda i, j: (0, 0))
    in_specs = [
        row, vec,
        pl.BlockSpec((d, tf), lambda i, j: (0, j)),
        pl.BlockSpec((d, tf), lambda i, j: (0, j + nf)),
        pl.BlockSpec((tf, d), lambda i, j: (j, 0)),
        vec,
    ]
    if post == "both":
        out_shape = (jax.ShapeDtypeStruct((t, d), F32), jax.ShapeDtypeStruct((t, d), BF16))
        out_specs = (row, row)
    else:
        out_shape = jax.ShapeDtypeStruct((t, d), F32)
        out_specs = row
    return pl.pallas_call(
        functools.partial(_ffn_body, post=post),
        grid=(t // tm, nf),
        in_specs=in_specs,
        out_specs=out_specs,
        out_shape=out_shape,
        scratch_shapes=[pltpu.VMEM((tm, d), BF16), pltpu.VMEM((tm, d), F32)],
        compiler_params=_params(("parallel", "arbitrary")),
        name="ffn",
    )(x, g, wi, wi, wo, g2)


def _conv_body(h_ref, wa_ref, wg_ref, cw_ref, cb_ref, lg_ref, lb_ref, out_ref, buf_ref, *, tm, tiles_per_seq):
    t = pl.program_id(0)

    @pl.when(t % tiles_per_seq == 0)
    def _():
        buf_ref[0:CONV_HALO, :] = jnp.zeros((CONV_HALO, CONV_WIDTH), F32)
        buf_ref[CONV_HALO + tm:, :] = jnp.zeros((SUBLANES, CONV_WIDTH), F32)

    h = h_ref[...]
    a = _dot(h, wa_ref[...])
    g = _dot(h, wg_ref[...])
    buf_ref[CONV_HALO:CONV_HALO + tm, :] = a * _sigmoid(g)

    first_tap = CONV_HALO - (CONV_K - 1)
    ext = CONV_ROWS + SUBLANES
    bias = cb_ref[...]
    ln_g = lg_ref[...]
    ln_b = lb_ref[...]

    def step(i, carry):
        base = pl.multiple_of(i * CONV_ROWS, CONV_ROWS)
        y = None
        for r in range(SUBLANES):
            part = None
            for o in range(r, first_tap + CONV_K, SUBLANES):
                if o < first_tap:
                    continue
                k = o - first_tap
                term = cw_ref[k:k + 1, :] * buf_ref[pl.ds(base + (o - r), ext), :]
                part = term if part is None else part + term
            shifted = part[r:r + CONV_ROWS, :]
            y = shifted if y is None else y + shifted
        y = y + bias
        mu = jnp.mean(y, axis=-1, keepdims=True)
        yc = y - mu
        var = jnp.mean(yc * yc, axis=-1, keepdims=True)
        yn = yc * lax.rsqrt(var + EPS) * ln_g + ln_b
        out_ref[pl.ds(base, CONV_ROWS), :] = (yn * _sigmoid(yn)).astype(out_ref.dtype)
        return carry

    lax.fori_loop(0, tm // CONV_ROWS, step, 0)
    buf_ref[0:CONV_HALO, :] = buf_ref[tm:tm + CONV_HALO, :]


def _conv_branch(h, wa, wg, cw, cb, lg, lb, *, seq, tm=512):
    t, d = h.shape
    c = CONV_WIDTH
    vec = pl.BlockSpec((1, c), lambda i: (0, 0))
    return pl.pallas_call(
        functools.partial(_conv_body, tm=tm, tiles_per_seq=seq // tm),
        grid=(t // tm,),
        in_specs=[
            pl.BlockSpec((tm, d), lambda i: (i, 0)),
            pl.BlockSpec((d, c), lambda i: (0, 0)),
            pl.BlockSpec((d, c), lambda i: (0, 0)),
            pl.BlockSpec((CONV_K, c), lambda i: (0, 0)),
            vec, vec, vec,
        ],
        out_specs=pl.BlockSpec((tm, c), lambda i: (i, 0)),
        out_shape=jax.ShapeDtypeStruct((t, c), BF16),
        scratch_shapes=[pltpu.VMEM((CONV_HALO + tm + SUBLANES, c), F32)],
        compiler_params=_params(("arbitrary",)),
        name="conv_branch",
    )(h, wa, wg, cw, cb, lg, lb)


def _gelu_tanh(x):
    c = np.float32(np.sqrt(2.0 / np.pi))
    return x * (0.5 * (1.0 + jnp.tanh(c * (x + 0.044715 * (x * x * x)))))


def _sgu_body(h_ref, wu_ref, wv_ref, lg_ref, lb_ref, ws_ref, bs_ref, out_ref, *, tm):
    h = h_ref[...]
    u = _gelu_tanh(_dot(h, wu_ref[...]))
    v = _gelu_tanh(_dot(h, wv_ref[...]))
    mu = jnp.mean(v, axis=-1, keepdims=True)
    vc = v - mu
    var = jnp.mean(vc * vc, axis=-1, keepdims=True)
    vn = (vc * lax.rsqrt(var + EPS) * lg_ref[...] + lb_ref[...]).astype(BF16)

    n_chunks = tm // SGU_CHUNK
    gw = SGU_WIDTH // SGU_GROUPS
    row = lax.broadcasted_iota(jnp.int32, (SGU_CHUNK, SGU_CHUNK), 0)
    col = lax.broadcasted_iota(jnp.int32, (SGU_CHUNK, SGU_CHUNK), 1)
    bias = bs_ref[...]
    mixed = [[None] * SGU_GROUPS for _ in range(n_chunks)]
    for g in range(SGU_GROUPS):
        w = jnp.where(col <= row, ws_ref[g], 0.0).astype(BF16)
        rhs = jnp.concatenate(
            [vn[n * SGU_CHUNK:(n + 1) * SGU_CHUNK, g * gw:(g + 1) * gw] for n in range(n_chunks)], axis=1)
        res = _dot(w, rhs)
        for n in range(n_chunks):
            mixed[n][g] = res[:, n * gw:(n + 1) * gw]
    for n in range(n_chunks):
        m = jnp.concatenate(mixed[n], axis=1) + bias
        rows = slice(n * SGU_CHUNK, (n + 1) * SGU_CHUNK)
        out_ref[rows, :] = (u[rows, :] * m).astype(out_ref.dtype)


def _sgu_branch(h, wu, wv, lg, lb, ws, bs_full, *, tm=512):
    t, d = h.shape
    c = SGU_WIDTH
    vec = pl.BlockSpec((1, c), lambda i: (0, 0))
    return pl.pallas_call(
        functools.partial(_sgu_body, tm=tm),
        grid=(t // tm,),
        in_specs=[
            pl.BlockSpec((tm, d), lambda i: (i, 0)),
            pl.BlockSpec((d, c), lambda i: (0, 0)),
            pl.BlockSpec((d, c), lambda i: (0, 0)),
            vec, vec,
            pl.BlockSpec((SGU_GROUPS, SGU_CHUNK, SGU_CHUNK), lambda i: (0, 0, 0)),
            pl.BlockSpec((SGU_CHUNK, c), lambda i: (0, 0)),
        ],
        out_specs=pl.BlockSpec((tm, c), lambda i: (i, 0)),
        out_shape=jax.ShapeDtypeStruct((t, c), BF16),
        compiler_params=_params(("parallel",)),
        name="sgu_branch",
    )(h, wu, wv, lg, lb, ws, bs_full)


def _qkv_body(h_ref, wq_ref, wk_ref, wv_ref, rc_ref, rs1_ref, rs2_ref, q_ref, k_ref, v_ref, km_ref):
    blk = pl.program_id(1)

    @pl.when(blk == 0)
    def _():
        km_ref[...] = jnp.zeros(km_ref.shape, F32)

    h = h_ref[...]
    q = _dot(h, wq_ref[...])
    k = _dot(h, wk_ref[...])
    v_ref[...] = _dot(h, wv_ref[...]).astype(v_ref.dtype)

    rc, rs1, rs2 = rc_ref[...], rs1_ref[...], rs2_ref[...]
    half = ROPE_DIM // 2

    def rope(x):
        return x * rc + pltpu.roll(x, LANES - half, 1) * rs1 + pltpu.roll(x, half, 1) * rs2

    rows = h.shape[0]
    lane = lax.broadcasted_iota(jnp.int32, (rows, LANES), 1)
    lane_f = lane.astype(F32)
    in_head = lane < HEAD_DIM
    own = lane == HEAD_DIM + blk
    past = jnp.logical_and(lane >= HEAD_DIM, lane < HEAD_DIM + blk)
    onehot = jnp.where(own, 1.0, 0.0)
    scale = HEAD_DIM ** -0.5
    neg_inf = float("-inf")

    for hd in range(ATT_HEADS):
        slab = slice((hd // 2) * LANES, (hd // 2 + 1) * LANES)
        qs = rope(q[:, slab])
        ks = rope(k[:, slab])
        if hd % 2 == 1:
            qs = pltpu.roll(qs, HEAD_DIM, 1)
            ks = pltpu.roll(ks, HEAD_DIM, 1)
        qh = jnp.where(in_head, qs * scale, 0.0)
        kh = jnp.where(in_head, ks, 0.0)

        gate = lax.dot_general(qh, km_ref[hd], (((1,), (1,)), ((), ())),
                               precision=lax.Precision.HIGHEST, preferred_element_type=F32)
        g = jnp.where(past, gate, neg_inf)
        bias = jnp.where(own, 0.0, MASK_VALUE)
        for _ in range(MOBA_TOPK):
            m = jnp.max(g, axis=-1, keepdims=True)
            idx = jnp.min(jnp.where(g == m, lane_f, 2.0 * LANES), axis=-1, keepdims=True)
            pick = jnp.logical_and(lane_f == idx, m > neg_inf)
            bias = jnp.where(pick, 0.0, bias)
            g = jnp.where(pick, neg_inf, g)

        q_ref[0, hd] = jnp.where(in_head, qh, bias).astype(q_ref.dtype)
        k_ref[0, hd] = jnp.where(in_head, kh, onehot).astype(k_ref.dtype)
        km_ref[hd, pl.ds(HEAD_DIM + blk, 1), :] = jnp.sum(kh, axis=0, keepdims=True) * (1.0 / MOBA_BLOCK)


def _qkv_route(h, wq, wk, wv, rc, rs1, rs2, *, batch, seq):
    t, d = h.shape
    tm = MOBA_BLOCK
    nb = seq // tm
    w_spec = pl.BlockSpec((d, ATT_WIDTH), lambda b, i: (0, 0))
    r_spec = pl.BlockSpec((tm, LANES), lambda b, i: (i, 0))
    aug_spec = pl.BlockSpec((1, ATT_HEADS, tm, LANES), lambda b, i: (b, 0, i, 0))
    aug_shape = jax.ShapeDtypeStruct((batch, ATT_HEADS, seq, LANES), BF16)
    return pl.pallas_call(
        _qkv_body,
        grid=(batch, nb),
        in_specs=[pl.BlockSpec((tm, d), lambda b, i: (b * nb + i, 0)), w_spec, w_spec, w_spec,
                  r_spec, r_spec, r_spec],
        out_specs=(aug_spec, aug_spec, pl.BlockSpec((tm, ATT_WIDTH), lambda b, i: (b * nb + i, 0))),
        out_shape=(aug_shape, aug_shape, jax.ShapeDtypeStruct((t, ATT_WIDTH), BF16)),
        scratch_shapes=[pltpu.VMEM((ATT_HEADS, LANES, LANES), F32)],
        compiler_params=_params(("arbitrary", "arbitrary")),
        name="qkv_route",
    )(h, wq, wk, wv, rc, rs1, rs2)


def _attn_body(q_ref, k_ref, v_ref, o_ref, m_ref, l_ref, acc_ref):
    blk = pl.program_id(2)
    tq = MOBA_BLOCK
    nt = (((1,), (1,)), ((), ()))
    row = lax.broadcasted_iota(jnp.int32, (tq, tq), 0)
    col = lax.broadcasted_iota(jnp.int32, (tq, tq), 1)
    causal = jnp.where(col <= row, 0.0, MASK_VALUE)
    own = pl.multiple_of(blk * tq, tq)
    v_own = v_ref[0, pl.ds(own, tq), :]

    for hh in range(2):
        s = lax.dot_general(q_ref[0, hh], k_ref[0, hh, pl.ds(own, tq), :], nt,
                            preferred_element_type=F32) + causal
        m = jnp.max(s, axis=-1, keepdims=True)
        p = jnp.exp(s - m)
        m_ref[hh] = jnp.broadcast_to(m, (tq, LANES))
        l_ref[hh] = jnp.broadcast_to(jnp.sum(p, axis=-1, keepdims=True), (tq, LANES))
        acc_ref[hh] = _dot(p.astype(BF16), v_own)

    def step(j, carry):
        start = pl.multiple_of(j * tq, tq)
        v_j = v_ref[0, pl.ds(start, tq), :]
        for hh in range(2):
            s = lax.dot_general(q_ref[0, hh], k_ref[0, hh, pl.ds(start, tq), :], nt,
                                preferred_element_type=F32)
            m_old = m_ref[hh][:, :1]
            m_new = jnp.maximum(m_old, jnp.max(s, axis=-1, keepdims=True))
            alpha = jnp.exp(m_old - m_new)
            p = jnp.exp(s - m_new)
            l_ref[hh] = jnp.broadcast_to(alpha * l_ref[hh][:, :1] + jnp.sum(p, axis=-1, keepdims=True),
                                         (tq, LANES))
            m_ref[hh] = jnp.broadcast_to(m_new, (tq, LANES))
            acc_ref[hh] = alpha * acc_ref[hh] + _dot(p.astype(BF16), v_j)
        return carry

    lax.fori_loop(0, blk, step, 0)

    lane = lax.broadcasted_iota(jnp.int32, (tq, LANES), 1)
    o0 = acc_ref[0] / l_ref[0]
    o1 = acc_ref[1] / l_ref[1]
    o_ref[0] = jnp.where(lane < HEAD_DIM, o0, o1).astype(o_ref.dtype)


def _moba_attention(q_aug, k_aug, v, *, batch, seq):
    tq = MOBA_BLOCK
    nb = seq // tq
    v3 = v.reshape(batch, seq, ATT_WIDTH)
    out = pl.pallas_call(
        _attn_body,
        grid=(batch, ATT_HEADS // 2, nb),
        in_specs=[
            pl.BlockSpec((1, 2, tq, LANES), lambda b, hp, i: (b, hp, i, 0)),
            pl.BlockSpec((1, 2, seq, LANES), lambda b, hp, i: (b, hp, 0, 0)),
            pl.BlockSpec((1, seq, LANES), lambda b, hp, i: (b, 0, hp)),
        ],
        out_specs=pl.BlockSpec((1, tq, LANES), lambda b, hp, i: (b, i, hp)),
        out_shape=jax.ShapeDtypeStruct((batch, seq, ATT_WIDTH), BF16),
        scratch_shapes=[pltpu.VMEM((2, tq, LANES), F32)] * 3,
        compiler_params=_params(("parallel", "parallel", "arbitrary")),
        name="moba_attention",
    )(q_aug, k_aug, v3)
    return out.reshape(batch * seq, ATT_WIDTH)


def _merge_body(x_ref, h_ref, ya_ref, yb_ref, yc_ref, wgate_ref, gb_ref, wbr_ref, wout_ref, out_ref):
    h = h_ref[...]
    d = x_ref.shape[1]
    merged = None
    for n, y_ref in enumerate((ya_ref, yb_ref, yc_ref)):
        gate = _sigmoid(_dot(h, wgate_ref[:, n * d:(n + 1) * d]) + gb_ref[:, n * d:(n + 1) * d])
        term = gate * _dot(y_ref[...], wbr_ref[n])
        merged = term if merged is None else merged + term
    out_ref[...] = x_ref[...] + _dot(merged.astype(BF16), wout_ref[...])


def _merge(x, h, ya, yb, yc, wgate, gb, wbr, wout, *, tm=512):
    t, d = x.shape
    c = CONV_WIDTH
    row_d = pl.BlockSpec((tm, d), lambda i: (i, 0))
    row_c = pl.BlockSpec((tm, c), lambda i: (i, 0))
    return pl.pallas_call(
        _merge_body,
        grid=(t // tm,),
        in_specs=[
            row_d, row_d, row_c, row_c, row_c,
            pl.BlockSpec((d, N_BRANCH * d), lambda i: (0, 0)),
            pl.BlockSpec((1, N_BRANCH * d), lambda i: (0, 0)),
            pl.BlockSpec((N_BRANCH, c, d), lambda i: (0, 0, 0)),
            pl.BlockSpec((d, d), lambda i: (0, 0)),
        ],
        out_specs=row_d,
        out_shape=jax.ShapeDtypeStruct((t, d), F32),
        compiler_params=_params(("parallel",)),
        name="merge",
    )(x, h, ya, yb, yc, wgate, gb, wbr, wout)


def _rope_lane_tables(seq):
    pos = jnp.arange(seq, dtype=F32)
    inv_freq = ROPE_THETA ** (-jnp.arange(0, ROPE_DIM, 2, dtype=F32) / ROPE_DIM)
    ang = pos[:, None] * inv_freq[None, :]
    cos, sin = jnp.cos(ang), jnp.sin(ang)
    half = ROPE_DIM // 2
    ones = jnp.ones((seq, HEAD_DIM - ROPE_DIM), F32)
    zeros = jnp.zeros((seq, HEAD_DIM - ROPE_DIM), F32)
    zh = jnp.zeros((seq, half), F32)
    rc = jnp.concatenate([cos, cos, ones], axis=1)
    rs1 = jnp.concatenate([-sin, zh, zeros], axis=1)
    rs2 = jnp.concatenate([zh, sin, zeros], axis=1)
    reps = LANES // HEAD_DIM
    return tuple(jnp.tile(a, (1, reps)) for a in (rc, rs1, rs2))


def kernel(x, ffn1_norm, ffn1_wi, ffn1_wo, mix_norm, w_in, conv_w, conv_b, conv_ln_g, conv_ln_b,
           sgu_ln_g, sgu_ln_b, sgu_w, sgu_b, w_branch, gate_b, w_out, ffn2_norm, ffn2_wi, ffn2_wo,
           final_norm):
    batch, seq, d = x.shape
    depth = ffn1_norm.shape[0]
    assert seq % MOBA_BLOCK == 0 and seq % 512 == 0 and d % LANES == 0
    xt = x.reshape(batch * seq, d)
    rc, rs1, rs2 = _rope_lane_tables(seq)
    gw = SGU_WIDTH // SGU_GROUPS

    o_sgu = 2 * CONV_WIDTH
    o_q = o_sgu + 2 * SGU_WIDTH
    o_k, o_v, o_gate = o_q + ATT_WIDTH, o_q + 2 * ATT_WIDTH, o_q + 3 * ATT_WIDTH

    def row(a):
        return a.reshape(1, -1)

    for l in range(depth):
        wl = w_in[l]

        def cols(a, b):
            return wl[:, a:b].astype(BF16)

        xt, h = _ffn(xt, row(ffn1_norm[l]), ffn1_wi[l].astype(BF16), ffn1_wo[l].astype(BF16),
                     row(mix_norm[l]), post="both")

        ya = _conv_branch(h, cols(0, CONV_WIDTH), cols(CONV_WIDTH, o_sgu), conv_w[l], row(conv_b[l]),
                          row(conv_ln_g[l]), row(conv_ln_b[l]), seq=seq)

        bs_full = jnp.repeat(sgu_b[l].T, gw, axis=1)
        yb = _sgu_branch(h, cols(o_sgu, o_sgu + SGU_WIDTH), cols(o_sgu + SGU_WIDTH, o_q),
                         row(sgu_ln_g[l]), row(sgu_ln_b[l]), sgu_w[l], bs_full)

        q_aug, k_aug, v = _qkv_route(h, cols(o_q, o_k), cols(o_k, o_v), cols(o_v, o_gate), rc, rs1, rs2,
                                     batch=batch, seq=seq)
        yc = _moba_attention(q_aug, k_aug, v, batch=batch, seq=seq)

        xt = _merge(xt, h, ya, yb, yc, cols(o_gate, wl.shape[1]), row(gate_b[l]),
                    w_branch[l].astype(BF16), w_out[l].astype(BF16))

        last = l == depth - 1
        xt = _ffn(xt, row(ffn2_norm[l]), ffn2_wi[l].astype(BF16), ffn2_wo[l].astype(BF16),
                  row(final_norm), post="norm_only" if last else "none")
    return xt.reshape(batch, seq, d)
```

```python
import functools

import numpy as np
import jax
import jax.numpy as jnp
from jax import lax
from jax.experimental import pallas as pl
from jax.experimental.pallas import tpu as pltpu

D_FF = 2816
CONV_WIDTH = 512
CONV_K = 31
SGU_WIDTH = 512
SGU_GROUPS = 4
SGU_CHUNK = 128
ATT_HEADS = 8
HEAD_DIM = 64
ATT_WIDTH = ATT_HEADS * HEAD_DIM
MOBA_BLOCK = 256
MOBA_TOPK = 3
ROPE_THETA = 500000.0
ROPE_DIM = HEAD_DIM // 4
N_BRANCH = 3
EPS = 1e-6

LANES = 128
SUBLANES = 8
VMEM_LIMIT = 56 * 1024 * 1024
MASK_VALUE = -1e30
CONV_HALO = 32
CONV_ROWS = 32
ATT_PAIRS = 2
KEY_STEP = 4
MAX_BLOCKS = 32
AUG = 2 * LANES

BF16 = jnp.bfloat16
F32 = jnp.float32
NT_DIMS = (((1,), (1,)), ((), ()))


def _dot(a, b):
    return jnp.dot(a, b, preferred_element_type=F32)


def _sigmoid(x):
    return 1.0 / (1.0 + jnp.exp(-x))


def _rms(x, g):
    return x * lax.rsqrt(jnp.mean(x * x, axis=-1, keepdims=True) + EPS) * g


def _params(sem, flags=None):
    return pltpu.CompilerParams(dimension_semantics=sem, vmem_limit_bytes=VMEM_LIMIT, flags=flags)


def _ffn_body(x_ref, g_ref, wg_ref, wu_ref, wo_ref, g2_ref, *refs, post):
    if post == "both":
        out_ref, nxt_ref, xn_ref, acc_ref = refs
    else:
        out_ref, xn_ref, acc_ref = refs
    j = pl.program_id(1)

    @pl.when(j == 0)
    def _():
        xn_ref[...] = _rms(x_ref[...], g_ref[...]).astype(BF16)

    xn = xn_ref[...]
    gate = _dot(xn, wg_ref[...])
    up = _dot(xn, wu_ref[...])
    act = (gate * _sigmoid(gate) * up).astype(BF16)
    part = _dot(act, wo_ref[...])

    @pl.when(j == 0)
    def _():
        acc_ref[...] = part

    @pl.when(j > 0)
    def _():
        acc_ref[...] += part

    @pl.when(j == pl.num_programs(1) - 1)
    def _():
        y = x_ref[...] + 0.5 * acc_ref[...]
        if post == "both":
            out_ref[...] = y
            nxt_ref[...] = _rms(y, g2_ref[...]).astype(nxt_ref.dtype)
        elif post == "norm_only":
            out_ref[...] = _rms(y, g2_ref[...])
        else:
            out_ref[...] = y


def _ffn(x, g, wi, wo, g2, *, post, tm=512, tf=1408):
    t, d = x.shape
    nf = D_FF // tf
    row = pl.BlockSpec((tm, d), lambda i, j: (i, 0))
    vec = pl.BlockSpec((1, d), lambda i, j: (0, 0))
    in_specs = [
        row, vec,
        pl.BlockSpec((d, tf), lambda i, j: (0, j)),
        pl.BlockSpec((d, tf), lambda i, j: (0, j + nf)),
        pl.BlockSpec((tf, d), lambda i, j: (j, 0)),
        vec,
    ]
    if post == "both":
        out_shape = (jax.ShapeDtypeStruct((t, d), F32), jax.ShapeDtypeStruct((t, d), BF16))
        out_specs = (row, row)
    else:
        out_shape = jax.ShapeDtypeStruct((t, d), F32)
        out_specs = row
    return pl.pallas_call(
        functools.partial(_ffn_body, post=post),
        grid=(t // tm, nf),
        in_specs=in_specs,
        out_specs=out_specs,
        out_shape=out_shape,
        scratch_shapes=[pltpu.VMEM((tm, d), BF16), pltpu.VMEM((tm, d), F32)],
        compiler_params=_params(("parallel", "arbitrary")),
        name="ffn",
    )(x, g, wi, wi, wo, g2)


def _conv_body(h_ref, wa_ref, wg_ref, cw_ref, cb_ref, lg_ref, lb_ref, out_ref, buf_ref, *, tm, tiles_per_seq):
    t = pl.program_id(0)

    @pl.when(t % tiles_per_seq == 0)
    def _():
        buf_ref[0:CONV_HALO, :] = jnp.zeros((CONV_HALO, CONV_WIDTH), F32)
        buf_ref[CONV_HALO + tm:, :] = jnp.zeros((SUBLANES, CONV_WIDTH), F32)

    h = h_ref[...]
    a = _dot(h, wa_ref[...])
    g = _dot(h, wg_ref[...])
    buf_ref[CONV_HALO:CONV_HALO + tm, :] = a * _sigmoid(g)

    first_tap = CONV_HALO - (CONV_K - 1)
    ext = CONV_ROWS + SUBLANES
    bias = cb_ref[...]
    ln_g = lg_ref[...]
    ln_b = lb_ref[...]

    def step(i, carry):
        base = pl.multiple_of(i * CONV_ROWS, CONV_ROWS)
        y = None
        for r in range(SUBLANES):
            part = None
            for o in range(r, first_tap + CONV_K, SUBLANES):
                if o < first_tap:
                    continue
                k = o - first_tap
                term = cw_ref[k:k + 1, :] * buf_ref[pl.ds(base + (o - r), ext), :]
                part = term if part is None else part + term
            shifted = part[r:r + CONV_ROWS, :]
            y = shifted if y is None else y + shifted
        y = y + bias
        mu = jnp.mean(y, axis=-1, keepdims=True)
        yc = y - mu
        var = jnp.mean(yc * yc, axis=-1, keepdims=True)
        yn = yc * lax.rsqrt(var + EPS) * ln_g + ln_b
        out_ref[pl.ds(base, CONV_ROWS), :] = (yn * _sigmoid(yn)).astype(out_ref.dtype)
        return carry

    lax.fori_loop(0, tm // CONV_ROWS, step, 0)
    buf_ref[0:CONV_HALO, :] = buf_ref[tm:tm + CONV_HALO, :]


def _conv_branch(h, wa, wg, cw, cb, lg, lb, *, seq, tm=512):
    t, d = h.shape
    c = CONV_WIDTH
    vec = pl.BlockSpec((1, c), lambda i: (0, 0))
    return pl.pallas_call(
        functools.partial(_conv_body, tm=tm, tiles_per_seq=seq // tm),
        grid=(t // tm,),
        in_specs=[
            pl.BlockSpec((tm, d), lambda i: (i, 0)),
            pl.BlockSpec((d, c), lambda i: (0, 0)),
            pl.BlockSpec((d, c), lambda i: (0, 0)),
            pl.BlockSpec((CONV_K, c), lambda i: (0, 0)),
            vec, vec, vec,
        ],
        out_specs=pl.BlockSpec((tm, c), lambda i: (i, 0)),
        out_shape=jax.ShapeDtypeStruct((t, c), BF16),
        scratch_shapes=[pltpu.VMEM((CONV_HALO + tm + SUBLANES, c), F32)],
        compiler_params=_params(("arbitrary",)),
        name="conv_branch",
    )(h, wa, wg, cw, cb, lg, lb)


def _gelu_tanh(x):
    c = np.float32(np.sqrt(2.0 / np.pi))
    return x * (0.5 * (1.0 + jnp.tanh(c * (x + 0.044715 * (x * x * x)))))


def _sgu_body(h_ref, wu_ref, wv_ref, lg_ref, lb_ref, ws_ref, bs_ref, out_ref, *, tm):
    h = h_ref[...]
    u = _gelu_tanh(_dot(h, wu_ref[...]))
    v = _gelu_tanh(_dot(h, wv_ref[...]))
    mu = jnp.mean(v, axis=-1, keepdims=True)
    vc = v - mu
    var = jnp.mean(vc * vc, axis=-1, keepdims=True)
    vn = (vc * lax.rsqrt(var + EPS) * lg_ref[...] + lb_ref[...]).astype(BF16)

    n_chunks = tm // SGU_CHUNK
    gw = SGU_WIDTH // SGU_GROUPS
    row = lax.broadcasted_iota(jnp.int32, (SGU_CHUNK, SGU_CHUNK), 0)
    col = lax.broadcasted_iota(jnp.int32, (SGU_CHUNK, SGU_CHUNK), 1)
    bias = bs_ref[...]
    mixed = [[None] * SGU_GROUPS for _ in range(n_chunks)]
    for g in range(SGU_GROUPS):
        w = jnp.where(col <= row, ws_ref[g], 0.0).astype(BF16)
        rhs = jnp.concatenate(
            [vn[n * SGU_CHUNK:(n + 1) * SGU_CHUNK, g * gw:(g + 1) * gw] for n in range(n_chunks)], axis=1)
        res = _dot(w, rhs)
        for n in range(n_chunks):
            mixed[n][g] = res[:, n * gw:(n + 1) * gw]
    for n in range(n_chunks):
        m = jnp.concatenate(mixed[n], axis=1) + bias
        rows = slice(n * SGU_CHUNK, (n + 1) * SGU_CHUNK)
        out_ref[rows, :] = (u[rows, :] * m).astype(out_ref.dtype)


def _sgu_branch(h, wu, wv, lg, lb, ws, bs_full, *, tm=512):
    t, d = h.shape
    c = SGU_WIDTH
    vec = pl.BlockSpec((1, c), lambda i: (0, 0))
    return pl.pallas_call(
        functools.partial(_sgu_body, tm=tm),
        grid=(t // tm,),
        in_specs=[
            pl.BlockSpec((tm, d), lambda i: (i, 0)),
            pl.BlockSpec((d, c), lambda i: (0, 0)),
            pl.BlockSpec((d, c), lambda i: (0, 0)),
            vec, vec,
            pl.BlockSpec((SGU_GROUPS, SGU_CHUNK, SGU_CHUNK), lambda i: (0, 0, 0)),
            pl.BlockSpec((SGU_CHUNK, c), lambda i: (0, 0)),
        ],
        out_specs=pl.BlockSpec((tm, c), lambda i: (i, 0)),
        out_shape=jax.ShapeDtypeStruct((t, c), BF16),
        compiler_params=_params(("parallel",)),
        name="sgu_branch",
    )(h, wu, wv, lg, lb, ws, bs_full)


def _qkv_body(h_ref, wqv_ref, wk_ref, rc_ref, rs1_ref, rs2_ref, ct_ref, st_ref, q_ref, k_ref, v_ref, km_ref):
    blk = pl.program_id(1)

    @pl.when(blk == 0)
    def _():
        km_ref[...] = jnp.zeros(km_ref.shape, F32)

    h = h_ref[...]
    qv = lax.dot_general(wqv_ref[...], h, NT_DIMS, preferred_element_type=F32)
    v_ref[0, 0] = qv[ATT_WIDTH:, :].astype(v_ref.dtype)
    k = _dot(h, wk_ref[...])

    rc, rs1, rs2 = rc_ref[...], rs1_ref[...], rs2_ref[...]
    cos_t, sin_t = ct_ref[...], st_ref[...]
    half = ROPE_DIM // 2
    tokens = h.shape[0]
    scale = HEAD_DIM ** -0.5
    neg_inf = float("-inf")

    lane = lax.broadcasted_iota(jnp.int32, (tokens, LANES), 1)
    km_lane = lax.broadcasted_iota(jnp.int32, (MAX_BLOCKS, LANES), 1)
    blk_row = lax.broadcasted_iota(jnp.int32, (MAX_BLOCKS, tokens), 0)
    blk_row_f = blk_row.astype(F32)
    onehot = jnp.where(lane == blk, 1.0, 0.0)
    zeros_head = jnp.zeros((HEAD_DIM, tokens), F32)
    zeros_pad = jnp.zeros((AUG - 2 * HEAD_DIM - MAX_BLOCKS, 2 * tokens), F32)

    def rope_t(x):
        x1, x2 = x[:half], x[half:ROPE_DIM]
        return jnp.concatenate([x1 * cos_t - x2 * sin_t, x2 * cos_t + x1 * sin_t, x[ROPE_DIM:]], axis=0)

    def route(gate):
        g = jnp.where(blk_row < blk, gate, neg_inf)
        bias = jnp.where(blk_row == blk, 0.0, MASK_VALUE)
        for _ in range(MOBA_TOPK):
            m = jnp.max(g, axis=0, keepdims=True)
            idx = jnp.min(jnp.where(g == m, blk_row_f, 2.0 * MAX_BLOCKS), axis=0, keepdims=True)
            pick = jnp.logical_and(blk_row_f == idx, m > neg_inf)
            bias = jnp.where(pick, 0.0, bias)
            g = jnp.where(pick, neg_inf, g)
        return bias

    for pair in range(ATT_HEADS // 2):
        even, odd = 2 * pair, 2 * pair + 1
        x = k[:, pair * LANES:(pair + 1) * LANES]
        ks = x * rc + pltpu.roll(x, LANES - half, 1) * rs1 + pltpu.roll(x, half, 1) * rs2
        k_ref[0, pair] = jnp.concatenate([ks, onehot], axis=1).astype(k_ref.dtype)

        q_even = rope_t(qv[even * HEAD_DIM:(even + 1) * HEAD_DIM, :]) * scale
        q_odd = rope_t(qv[odd * HEAD_DIM:(odd + 1) * HEAD_DIM, :]) * scale
        q_pair = jnp.concatenate([q_even, q_odd], axis=0)

        km = km_ref[pair]
        gate_even = jnp.dot(jnp.where(km_lane < HEAD_DIM, km, 0.0), q_pair,
                            precision=lax.Precision.HIGHEST, preferred_element_type=F32)
        gate_odd = jnp.dot(jnp.where(km_lane >= HEAD_DIM, km, 0.0), q_pair,
                           precision=lax.Precision.HIGHEST, preferred_element_type=F32)
        q_ref[0, pair, 0] = jnp.concatenate([
            jnp.concatenate([q_even, zeros_head], axis=1),
            jnp.concatenate([zeros_head, q_odd], axis=1),
            jnp.concatenate([route(gate_even), route(gate_odd)], axis=1),
            zeros_pad], axis=0).astype(q_ref.dtype)

        km_ref[pair, pl.ds(blk, 1), :] = jnp.sum(ks, axis=0, keepdims=True) * (1.0 / MOBA_BLOCK)


def _qkv_route(h, wqv_t, wk, rc, rs1, rs2, cos_t, sin_t, *, batch, seq):
    t, d = h.shape
    tm = MOBA_BLOCK
    nb = seq // tm
    pairs = ATT_HEADS // 2
    r_spec = pl.BlockSpec((tm, LANES), lambda b, i: (i, 0))
    t_spec = pl.BlockSpec((ROPE_DIM // 2, tm), lambda b, i: (0, i))
    return pl.pallas_call(
        _qkv_body,
        grid=(batch, nb),
        in_specs=[pl.BlockSpec((tm, d), lambda b, i: (b * nb + i, 0)),
                  pl.BlockSpec((2 * ATT_WIDTH, d), lambda b, i: (0, 0)),
                  pl.BlockSpec((d, ATT_WIDTH), lambda b, i: (0, 0)),
                  r_spec, r_spec, r_spec, t_spec, t_spec],
        out_specs=(pl.BlockSpec((1, pairs, 1, AUG, 2 * tm), lambda b, i: (b, 0, i, 0, 0)),
                   pl.BlockSpec((1, pairs, tm, AUG), lambda b, i: (b, 0, i, 0)),
                   pl.BlockSpec((1, 1, ATT_WIDTH, tm), lambda b, i: (b, i, 0, 0))),
        out_shape=(jax.ShapeDtypeStruct((batch, pairs, nb, AUG, 2 * tm), BF16),
                   jax.ShapeDtypeStruct((batch, pairs, seq, AUG), BF16),
                   jax.ShapeDtypeStruct((batch, nb, ATT_WIDTH, tm), BF16)),
        scratch_shapes=[pltpu.VMEM((ATT_HEADS // 2, MAX_BLOCKS, LANES), F32)],
        compiler_params=_params(("arbitrary", "arbitrary")),
        name="qkv_route",
    )(h, wqv_t, wk, rc, rs1, rs2, cos_t, sin_t)


def _attn_body(q_ref, k_ref, v_ref, o_ref, m_ref, l_ref, acc_ref):
    blk = pl.program_id(2)
    tq = MOBA_BLOCK
    key = lax.broadcasted_iota(jnp.int32, (tq, 2 * tq), 0)
    lane = lax.broadcasted_iota(jnp.int32, (tq, 2 * tq), 1)
    qry = jnp.where(lane < tq, lane, lane - tq)
    causal = jnp.where(key <= qry, 0.0, MASK_VALUE)

    def scores(pp, first_blk, n_blk):
        start = pl.multiple_of(first_blk * tq, tq)
        return _dot(k_ref[0, pp, pl.ds(start, n_blk * tq), :], q_ref[0, pp, 0])

    def update(pp, s, first_blk, n_blk, init):
        v_t = jnp.concatenate([v_ref[0, first_blk + a, pp * LANES:(pp + 1) * LANES, :]
                               for a in range(n_blk)], axis=1)
        m_cur = jnp.max(s, axis=0, keepdims=True)
        if init:
            p = jnp.exp(s - m_cur)
            m_ref[pp] = m_cur
            l_ref[pp] = jnp.sum(p, axis=0, keepdims=True)
            acc_ref[pp] = _dot(v_t, p.astype(BF16))
        else:
            m_old = m_ref[pp]
            m_new = jnp.maximum(m_old, m_cur)
            alpha = jnp.exp(m_old - m_new)
            p = jnp.exp(s - m_new)
            l_ref[pp] = alpha * l_ref[pp] + jnp.sum(p, axis=0, keepdims=True)
            m_ref[pp] = m_new
            acc_ref[pp] = alpha * acc_ref[pp] + _dot(v_t, p.astype(BF16))

    def attend(first_blk, n_blk, mask, init):
        ss = [scores(pp, first_blk, n_blk) for pp in range(ATT_PAIRS)]
        for pp in range(ATT_PAIRS):
            update(pp, ss[pp] if mask is None else ss[pp] + mask, first_blk, n_blk, init)

    n_full = blk // KEY_STEP

    attend(blk, 1, causal, True)

    def step(j, carry):
        attend(j * KEY_STEP, KEY_STEP, None, False)
        return carry

    lax.fori_loop(0, n_full, step, 0)

    done = n_full * KEY_STEP
    width = KEY_STEP // 2
    while width >= 1:
        first = done
        n_piece = width

        @pl.when((blk & width) != 0)
        def _():
            attend(first, n_piece, None, False)

        done = done + (blk & width)
        width //= 2

    for pp in range(ATT_PAIRS):
        inv_l = 1.0 / l_ref[pp]
        o = acc_ref[pp] * inv_l
        o_t = jnp.concatenate([o[:HEAD_DIM, :tq], o[HEAD_DIM:, tq:]], axis=0)
        o_ref[0, :, pp * LANES:(pp + 1) * LANES] = o_t.T.astype(o_ref.dtype)


def _moba_attention(q_aug, k_aug, v_t, *, batch, seq):
    tq = MOBA_BLOCK
    nb = seq // tq
    gw = ATT_PAIRS * LANES
    out = pl.pallas_call(
        _attn_body,
        grid=(batch, ATT_HEADS // (2 * ATT_PAIRS), nb),
        in_specs=[
            pl.BlockSpec((1, ATT_PAIRS, 1, AUG, 2 * tq), lambda b, g, i: (b, g, i, 0, 0)),
            pl.BlockSpec((1, ATT_PAIRS, seq, AUG), lambda b, g, i: (b, g, 0, 0)),
            pl.BlockSpec((1, nb, gw, tq), lambda b, g, i: (b, 0, g, 0)),
        ],
        out_specs=pl.BlockSpec((1, tq, gw), lambda b, g, i: (b, i, g)),
        out_shape=jax.ShapeDtypeStruct((batch, seq, ATT_WIDTH), BF16),
        scratch_shapes=[pltpu.VMEM((ATT_PAIRS, 1, 2 * tq), F32), pltpu.VMEM((ATT_PAIRS, 1, 2 * tq), F32),
                        pltpu.VMEM((ATT_PAIRS, LANES, 2 * tq), F32)],
        compiler_params=_params(("parallel", "parallel", "arbitrary")),
        name="moba_attention",
    )(q_aug, k_aug, v_t)
    return out.reshape(batch * seq, ATT_WIDTH)


def _merge_body(x_ref, h_ref, ya_ref, yb_ref, yc_ref, wgate_ref, gb_ref, wbr_ref, wout_ref, out_ref):
    h = h_ref[...]
    d = x_ref.shape[1]
    merged = None
    for n, y_ref in enumerate((ya_ref, yb_ref, yc_ref)):
        gate = _sigmoid(_dot(h, wgate_ref[:, n * d:(n + 1) * d]) + gb_ref[:, n * d:(n + 1) * d])
        term = gate * _dot(y_ref[...], wbr_ref[n])
        merged = term if merged is None else merged + term
    out_ref[...] = x_ref[...] + _dot(merged.astype(BF16), wout_ref[...])


def _merge(x, h, ya, yb, yc, wgate, gb, wbr, wout, *, tm=512):
    t, d = x.shape
    c = CONV_WIDTH
    row_d = pl.BlockSpec((tm, d), lambda i: (i, 0))
    row_c = pl.BlockSpec((tm, c), lambda i: (i, 0))
    return pl.pallas_call(
        _merge_body,
        grid=(t // tm,),
        in_specs=[
            row_d, row_d, row_c, row_c, row_c,
            pl.BlockSpec((d, N_BRANCH * d), lambda i: (0, 0)),
            pl.BlockSpec((1, N_BRANCH * d), lambda i: (0, 0)),
            pl.BlockSpec((N_BRANCH, c, d), lambda i: (0, 0, 0)),
            pl.BlockSpec((d, d), lambda i: (0, 0)),
        ],
        out_specs=row_d,
        out_shape=jax.ShapeDtypeStruct((t, d), F32),
        compiler_params=_params(("parallel",)),
        name="merge",
    )(x, h, ya, yb, yc, wgate, gb, wbr, wout)


def _rope_tables(seq):
    pos = jnp.arange(seq, dtype=F32)
    inv_freq = ROPE_THETA ** (-jnp.arange(0, ROPE_DIM, 2, dtype=F32) / ROPE_DIM)
    ang = pos[:, None] * inv_freq[None, :]
    cos, sin = jnp.cos(ang), jnp.sin(ang)
    half = ROPE_DIM // 2
    ones = jnp.ones((seq, HEAD_DIM - ROPE_DIM), F32)
    zeros = jnp.zeros((seq, HEAD_DIM - ROPE_DIM), F32)
    zh = jnp.zeros((seq, half), F32)
    rc = jnp.concatenate([cos, cos, ones], axis=1)
    rs1 = jnp.concatenate([-sin, zh, zeros], axis=1)
    rs2 = jnp.concatenate([zh, sin, zeros], axis=1)
    reps = LANES // HEAD_DIM
    return tuple(jnp.tile(a, (1, reps)) for a in (rc, rs1, rs2)) + (cos.T, sin.T)


def kernel(x, ffn1_norm, ffn1_wi, ffn1_wo, mix_norm, w_in, conv_w, conv_b, conv_ln_g, conv_ln_b,
           sgu_ln_g, sgu_ln_b, sgu_w, sgu_b, w_branch, gate_b, w_out, ffn2_norm, ffn2_wi, ffn2_wo,
           final_norm):
    batch, seq, d = x.shape
    depth = ffn1_norm.shape[0]
    assert seq % MOBA_BLOCK == 0 and seq % 512 == 0 and d % LANES == 0
    assert seq // MOBA_BLOCK <= MAX_BLOCKS
    xt = x.reshape(batch * seq, d)
    rc, rs1, rs2, cos_t, sin_t = _rope_tables(seq)
    gw = SGU_WIDTH // SGU_GROUPS

    o_sgu = 2 * CONV_WIDTH
    o_q = o_sgu + 2 * SGU_WIDTH
    o_k, o_v, o_gate = o_q + ATT_WIDTH, o_q + 2 * ATT_WIDTH, o_q + 3 * ATT_WIDTH

    def row(a):
        return a.reshape(1, -1)

    for l in range(depth):
        wl = w_in[l]

        def cols(a, b):
            return wl[:, a:b].astype(BF16)

        xt, h = _ffn(xt, row(ffn1_norm[l]), ffn1_wi[l].astype(BF16), ffn1_wo[l].astype(BF16),
                     row(mix_norm[l]), post="both")

        ya = _conv_branch(h, cols(0, CONV_WIDTH), cols(CONV_WIDTH, o_sgu), conv_w[l], row(conv_b[l]),
                          row(conv_ln_g[l]), row(conv_ln_b[l]), seq=seq)

        bs_full = jnp.repeat(sgu_b[l].T, gw, axis=1)
        yb = _sgu_branch(h, cols(o_sgu, o_sgu + SGU_WIDTH), cols(o_sgu + SGU_WIDTH, o_q),
                         row(sgu_ln_g[l]), row(sgu_ln_b[l]), sgu_w[l], bs_full)

        wqv_t = jnp.concatenate([wl[:, o_q:o_k], wl[:, o_v:o_gate]], axis=1).T.astype(BF16)
        q_aug_t, k_aug, v_t = _qkv_route(h, wqv_t, cols(o_k, o_v), rc, rs1, rs2, cos_t, sin_t,
                                         batch=batch, seq=seq)
        yc = _moba_attention(q_aug_t, k_aug, v_t, batch=batch, seq=seq)

        xt = _merge(xt, h, ya, yb, yc, cols(o_gate, wl.shape[1]), row(gate_b[l]),
                    w_branch[l].astype(BF16), w_out[l].astype(BF16))

        last = l == depth - 1
        xt = _ffn(xt, row(ffn2_norm[l]), ffn2_wi[l].astype(BF16), ffn2_wo[l].astype(BF16),
                  row(final_norm), post="norm_only" if last else "none")
    return xt.reshape(batch, seq, d)
```

```python
import functools

import numpy as np
import jax
import jax.numpy as jnp
from jax import lax
from jax.experimental import pallas as pl
from jax.experimental.pallas import tpu as pltpu

D_FF = 2816
CONV_WIDTH = 512
CONV_K = 31
SGU_WIDTH = 512
SGU_GROUPS = 4
SGU_CHUNK = 128
ATT_HEADS = 8
HEAD_DIM = 64
ATT_WIDTH = ATT_HEADS * HEAD_DIM
MOBA_BLOCK = 256
MOBA_TOPK = 3
ROPE_THETA = 500000.0
ROPE_DIM = HEAD_DIM // 4
N_BRANCH = 3
EPS = 1e-6

LANES = 128
SUBLANES = 8
VMEM_LIMIT = 56 * 1024 * 1024
MASK_VALUE = -1e30
CONV_HALO = 32
CONV_ROWS = 32
ATT_PAIRS = 2
KEY_STEP = 4
MAX_BLOCKS = 32
AUG = 2 * LANES

BF16 = jnp.bfloat16
F32 = jnp.float32
NT_DIMS = (((1,), (1,)), ((), ()))


def _dot(a, b):
    return jnp.dot(a, b, preferred_element_type=F32)


def _sigmoid(x):
    return 1.0 / (1.0 + jnp.exp(-x))


def _rms(x, g):
    return x * lax.rsqrt(jnp.mean(x * x, axis=-1, keepdims=True) + EPS) * g


def _params(sem, flags=None):
    return pltpu.CompilerParams(dimension_semantics=sem, vmem_limit_bytes=VMEM_LIMIT, flags=flags)


def _ffn_body(x_ref, g_ref, wg_ref, wu_ref, wo_ref, g2_ref, *refs, post, single_step):
    if post == "both":
        out_ref, nxt_ref = refs[:2]
    else:
        out_ref = refs[0]

    def finish(acc):
        y = x_ref[...] + 0.5 * acc
        if post == "both":
            out_ref[...] = y
            nxt_ref[...] = _rms(y, g2_ref[...]).astype(nxt_ref.dtype)
        elif post == "norm_only":
            out_ref[...] = _rms(y, g2_ref[...])
        else:
            out_ref[...] = y

    def partial_out(xn):
        gate = _dot(xn, wg_ref[...])
        up = _dot(xn, wu_ref[...])
        act = (gate * _sigmoid(gate) * up).astype(BF16)
        return _dot(act, wo_ref[...])

    if single_step:
        finish(partial_out(_rms(x_ref[...], g_ref[...]).astype(BF16)))
        return

    xn_ref, acc_ref = refs[-2:]
    j = pl.program_id(1)

    @pl.when(j == 0)
    def _():
        xn_ref[...] = _rms(x_ref[...], g_ref[...]).astype(BF16)

    part = partial_out(xn_ref[...])

    @pl.when(j == 0)
    def _():
        acc_ref[...] = part

    @pl.when(j > 0)
    def _():
        acc_ref[...] += part

    @pl.when(j == pl.num_programs(1) - 1)
    def _():
        finish(acc_ref[...])


def _ffn(x, g, wi, wo, g2, *, post, tm=512, tf=D_FF):
    t, d = x.shape
    nf = D_FF // tf
    single = nf == 1
    w_mode = dict(pipeline_mode=pl.Buffered(1)) if single else {}
    row = pl.BlockSpec((tm, d), lambda i, j: (i, 0))
    vec = pl.BlockSpec((1, d), lambda i, j: (0, 0))
    in_specs = [
        row, vec,
        pl.BlockSpec((d, tf), lambda i, j: (0, j), **w_mode),
        pl.BlockSpec((d, tf), lambda i, j: (0, j + nf), **w_mode),
        pl.BlockSpec((tf, d), lambda i, j: (j, 0), **w_mode),
        vec,
    ]
    if post == "both":
        out_shape = (jax.ShapeDtypeStruct((t, d), F32), jax.ShapeDtypeStruct((t, d), BF16))
        out_specs = (row, row)
    else:
        out_shape = jax.ShapeDtypeStruct((t, d), F32)
        out_specs = row
    return pl.pallas_call(
        functools.partial(_ffn_body, post=post, single_step=single),
        grid=(t // tm, nf),
        in_specs=in_specs,
        out_specs=out_specs,
        out_shape=out_shape,
        scratch_shapes=[] if single else [pltpu.VMEM((tm, d), BF16), pltpu.VMEM((tm, d), F32)],
        compiler_params=_params(("parallel", "arbitrary")),
        name="ffn",
    )(x, g, wi, wi, wo, g2)


def _conv_body(h_ref, wa_ref, wg_ref, cw_ref, cb_ref, lg_ref, lb_ref, out_ref, buf_ref, conv_ref, *, tm,
               tiles_per_seq):
    t = pl.program_id(0)

    @pl.when(t % tiles_per_seq == 0)
    def _():
        buf_ref[...] = jnp.zeros(buf_ref.shape, F32)

    h = h_ref[...]
    a = _dot(h, wa_ref[...])
    g = _dot(h, wg_ref[...])
    glu = a * _sigmoid(g)
    for r in range(SUBLANES):
        buf_ref[r, CONV_HALO - r:CONV_HALO - r + tm, :] = glu

    first_tap = CONV_HALO - (CONV_K - 1)

    def step(i, carry):
        base = pl.multiple_of(i * CONV_ROWS, CONV_ROWS)
        y = None
        for k in range(CONV_K):
            o = first_tap + k
            term = cw_ref[k:k + 1, :] * buf_ref[o % SUBLANES, pl.ds(base + (o - o % SUBLANES), CONV_ROWS), :]
            y = term if y is None else y + term
        conv_ref[pl.ds(base, CONV_ROWS), :] = y
        return carry

    lax.fori_loop(0, tm // CONV_ROWS, step, 0)
    for r in range(SUBLANES):
        buf_ref[r, 0:CONV_HALO, :] = buf_ref[r, tm:tm + CONV_HALO, :]

    y = conv_ref[...] + cb_ref[...]
    mu = jnp.mean(y, axis=-1, keepdims=True)
    yc = y - mu
    var = jnp.mean(yc * yc, axis=-1, keepdims=True)
    yn = yc * lax.rsqrt(var + EPS) * lg_ref[...] + lb_ref[...]
    out_ref[...] = (yn * _sigmoid(yn)).astype(out_ref.dtype)


def _conv_branch(h, wa, wg, cw, cb, lg, lb, *, seq, tm=512):
    t, d = h.shape
    c = CONV_WIDTH
    vec = pl.BlockSpec((1, c), lambda i: (0, 0))
    return pl.pallas_call(
        functools.partial(_conv_body, tm=tm, tiles_per_seq=seq // tm),
        grid=(t // tm,),
        in_specs=[
            pl.BlockSpec((tm, d), lambda i: (i, 0)),
            pl.BlockSpec((d, c), lambda i: (0, 0)),
            pl.BlockSpec((d, c), lambda i: (0, 0)),
            pl.BlockSpec((CONV_K, c), lambda i: (0, 0)),
            vec, vec, vec,
        ],
        out_specs=pl.BlockSpec((tm, c), lambda i: (i, 0)),
        out_shape=jax.ShapeDtypeStruct((t, c), BF16),
        scratch_shapes=[pltpu.VMEM((SUBLANES, CONV_HALO + tm, c), F32), pltpu.VMEM((tm, c), F32)],
        compiler_params=_params(("arbitrary",)),
        name="conv_branch",
    )(h, wa, wg, cw, cb, lg, lb)


def _gelu_tanh(x):
    c = np.float32(np.sqrt(2.0 / np.pi))
    return x * (0.5 * (1.0 + jnp.tanh(c * (x + 0.044715 * (x * x * x)))))


def _sgu_body(h_ref, wu_ref, wv_ref, lg_ref, lb_ref, ws_ref, bs_ref, out_ref, *, tm):
    h = h_ref[...]
    u = _gelu_tanh(_dot(h, wu_ref[...]))
    v = _gelu_tanh(_dot(h, wv_ref[...]))
    mu = jnp.mean(v, axis=-1, keepdims=True)
    vc = v - mu
    var = jnp.mean(vc * vc, axis=-1, keepdims=True)
    vn = (vc * lax.rsqrt(var + EPS) * lg_ref[...] + lb_ref[...]).astype(BF16)

    n_chunks = tm // SGU_CHUNK
    gw = SGU_WIDTH // SGU_GROUPS
    row = lax.broadcasted_iota(jnp.int32, (SGU_CHUNK, SGU_CHUNK), 0)
    col = lax.broadcasted_iota(jnp.int32, (SGU_CHUNK, SGU_CHUNK), 1)
    bias = bs_ref[...]
    mixed = [[None] * SGU_GROUPS for _ in range(n_chunks)]
    for g in range(SGU_GROUPS):
        w = jnp.where(col <= row, ws_ref[g], 0.0).astype(BF16)
        rhs = jnp.concatenate(
            [vn[n * SGU_CHUNK:(n + 1) * SGU_CHUNK, g * gw:(g + 1) * gw] for n in range(n_chunks)], axis=1)
        res = _dot(w, rhs)
        for n in range(n_chunks):
            mixed[n][g] = res[:, n * gw:(n + 1) * gw]
    for n in range(n_chunks):
        m = jnp.concatenate(mixed[n], axis=1) + bias
        rows = slice(n * SGU_CHUNK, (n + 1) * SGU_CHUNK)
        out_ref[rows, :] = (u[rows, :] * m).astype(out_ref.dtype)


def _sgu_branch(h, wu, wv, lg, lb, ws, bs_full, *, tm=512):
    t, d = h.shape
    c = SGU_WIDTH
    vec = pl.BlockSpec((1, c), lambda i: (0, 0))
    return pl.pallas_call(
        functools.partial(_sgu_body, tm=tm),
        grid=(t // tm,),
        in_specs=[
            pl.BlockSpec((tm, d), lambda i: (i, 0)),
            pl.BlockSpec((d, c), lambda i: (0, 0)),
            pl.BlockSpec((d, c), lambda i: (0, 0)),
            vec, vec,
            pl.BlockSpec((SGU_GROUPS, SGU_CHUNK, SGU_CHUNK), lambda i: (0, 0, 0)),
            pl.BlockSpec((SGU_CHUNK, c), lambda i: (0, 0)),
        ],
        out_specs=pl.BlockSpec((tm, c), lambda i: (i, 0)),
        out_shape=jax.ShapeDtypeStruct((t, c), BF16),
        compiler_params=_params(("parallel",)),
        name="sgu_branch",
    )(h, wu, wv, lg, lb, ws, bs_full)


def _qkv_body(h_ref, wqv_ref, wk_ref, rc_ref, rs1_ref, rs2_ref, ct_ref, st_ref, q_ref, k_ref, v_ref, km_ref):
    blk = pl.program_id(1)

    @pl.when(blk == 0)
    def _():
        km_ref[...] = jnp.zeros(km_ref.shape, F32)

    h = h_ref[...]
    qv = lax.dot_general(wqv_ref[...], h, NT_DIMS, preferred_element_type=F32)
    v_ref[0, 0] = qv[ATT_WIDTH:, :].astype(v_ref.dtype)
    k = _dot(h, wk_ref[...])

    rc, rs1, rs2 = rc_ref[...], rs1_ref[...], rs2_ref[...]
    cos_t, sin_t = ct_ref[...], st_ref[...]
    half = ROPE_DIM // 2
    tokens = h.shape[0]
    scale = HEAD_DIM ** -0.5
    neg_inf = float("-inf")

    lane = lax.broadcasted_iota(jnp.int32, (tokens, LANES), 1)
    km_lane = lax.broadcasted_iota(jnp.int32, (MAX_BLOCKS, LANES), 1)
    blk_row = lax.broadcasted_iota(jnp.int32, (MAX_BLOCKS, tokens), 0)
    blk_row_f = blk_row.astype(F32)
    onehot = jnp.where(lane == blk, 1.0, 0.0)
    zeros_head = jnp.zeros((HEAD_DIM, tokens), F32)
    zeros_pad = jnp.zeros((AUG - 2 * HEAD_DIM - MAX_BLOCKS, 2 * tokens), F32)

    def rope_t(x):
        x1, x2 = x[:half], x[half:ROPE_DIM]
        return jnp.concatenate([x1 * cos_t - x2 * sin_t, x2 * cos_t + x1 * sin_t, x[ROPE_DIM:]], axis=0)

    def route(gate):
        g = jnp.where(blk_row < blk, gate, neg_inf)
        bias = jnp.where(blk_row == blk, 0.0, MASK_VALUE)
        for _ in range(MOBA_TOPK):
            m = jnp.max(g, axis=0, keepdims=True)
            idx = jnp.min(jnp.where(g == m, blk_row_f, 2.0 * MAX_BLOCKS), axis=0, keepdims=True)
            pick = jnp.logical_and(blk_row_f == idx, m > neg_inf)
            bias = jnp.where(pick, 0.0, bias)
            g = jnp.where(pick, neg_inf, g)
        return bias

    for pair in range(ATT_HEADS // 2):
        even, odd = 2 * pair, 2 * pair + 1
        x = k[:, pair * LANES:(pair + 1) * LANES]
        ks = x * rc + pltpu.roll(x, LANES - half, 1) * rs1 + pltpu.roll(x, half, 1) * rs2
        k_ref[0, pair] = jnp.concatenate([ks, onehot], axis=1).astype(k_ref.dtype)

        q_even = rope_t(qv[even * HEAD_DIM:(even + 1) * HEAD_DIM, :]) * scale
        q_odd = rope_t(qv[odd * HEAD_DIM:(odd + 1) * HEAD_DIM, :]) * scale
        q_pair = jnp.concatenate([q_even, q_odd], axis=0)

        km = km_ref[pair]
        gate_even = jnp.dot(jnp.where(km_lane < HEAD_DIM, km, 0.0), q_pair,
                            precision=lax.Precision.HIGHEST, preferred_element_type=F32)
        gate_odd = jnp.dot(jnp.where(km_lane >= HEAD_DIM, km, 0.0), q_pair,
                           precision=lax.Precision.HIGHEST, preferred_element_type=F32)
        q_ref[0, pair, 0] = jnp.concatenate([
            jnp.concatenate([q_even, zeros_head], axis=1),
            jnp.concatenate([zeros_head, q_odd], axis=1),
            jnp.concatenate([route(gate_even), route(gate_odd)], axis=1),
            zeros_pad], axis=0).astype(q_ref.dtype)

        km_ref[pair, pl.ds(blk, 1), :] = jnp.sum(ks, axis=0, keepdims=True) * (1.0 / MOBA_BLOCK)


def _qkv_route(h, wqv_t, wk, rc, rs1, rs2, cos_t, sin_t, *, batch, seq):
    t, d = h.shape
    tm = MOBA_BLOCK
    nb = seq // tm
    pairs = ATT_HEADS // 2
    r_spec = pl.BlockSpec((tm, LANES), lambda b, i: (i, 0))
    t_spec = pl.BlockSpec((ROPE_DIM // 2, tm), lambda b, i: (0, i))
    return pl.pallas_call(
        _qkv_body,
        grid=(batch, nb),
        in_specs=[pl.BlockSpec((tm, d), lambda b, i: (b * nb + i, 0)),
                  pl.BlockSpec((2 * ATT_WIDTH, d), lambda b, i: (0, 0)),
                  pl.BlockSpec((d, ATT_WIDTH), lambda b, i: (0, 0)),
                  r_spec, r_spec, r_spec, t_spec, t_spec],
        out_specs=(pl.BlockSpec((1, pairs, 1, AUG, 2 * tm), lambda b, i: (b, 0, i, 0, 0)),
                   pl.BlockSpec((1, pairs, tm, AUG), lambda b, i: (b, 0, i, 0)),
                   pl.BlockSpec((1, 1, ATT_WIDTH, tm), lambda b, i: (b, i, 0, 0))),
        out_shape=(jax.ShapeDtypeStruct((batch, pairs, nb, AUG, 2 * tm), BF16),
                   jax.ShapeDtypeStruct((batch, pairs, seq, AUG), BF16),
                   jax.ShapeDtypeStruct((batch, nb, ATT_WIDTH, tm), BF16)),
        scratch_shapes=[pltpu.VMEM((ATT_HEADS // 2, MAX_BLOCKS, LANES), F32)],
        compiler_params=_params(("arbitrary", "arbitrary")),
        name="qkv_route",
    )(h, wqv_t, wk, rc, rs1, rs2, cos_t, sin_t)


def _attn_body(q_ref, k_ref, v_ref, o_ref, m_ref, l_ref, acc_ref, sa_ref, sb_ref):
    blk = pl.program_id(2)
    tq = MOBA_BLOCK
    key = lax.broadcasted_iota(jnp.int32, (tq, 2 * tq), 0)
    lane = lax.broadcasted_iota(jnp.int32, (tq, 2 * tq), 1)
    qry = jnp.where(lane < tq, lane, lane - tq)
    causal = jnp.where(key <= qry, 0.0, MASK_VALUE)

    def scores(pp, first_blk, n_blk):
        start = pl.multiple_of(first_blk * tq, tq)
        return _dot(k_ref[0, pp, pl.ds(start, n_blk * tq), :], q_ref[0, pp, 0])

    def update(pp, s, first_blk, n_blk, init):
        v_t = jnp.concatenate([v_ref[0, first_blk + a, pp * LANES:(pp + 1) * LANES, :]
                               for a in range(n_blk)], axis=1)
        m_cur = jnp.max(s, axis=0, keepdims=True)
        if init:
            p = jnp.exp(s - m_cur)
            m_ref[pp] = m_cur
            l_ref[pp] = jnp.sum(p, axis=0, keepdims=True)
            acc_ref[pp] = _dot(v_t, p.astype(BF16))
        else:
            m_old = m_ref[pp]
            m_new = jnp.maximum(m_old, m_cur)
            alpha = jnp.exp(m_old - m_new)
            p = jnp.exp(s - m_new)
            l_ref[pp] = alpha * l_ref[pp] + jnp.sum(p, axis=0, keepdims=True)
            m_ref[pp] = m_new
            acc_ref[pp] = alpha * acc_ref[pp] + _dot(v_t, p.astype(BF16))

    def attend(first_blk, n_blk, mask, init):
        ss = [scores(pp, first_blk, n_blk) for pp in range(ATT_PAIRS)]
        for pp in range(ATT_PAIRS):
            update(pp, ss[pp] if mask is None else ss[pp] + mask, first_blk, n_blk, init)

    n_full = blk // KEY_STEP

    def fill(buf, grp):
        ss = [scores(pp, grp * KEY_STEP, KEY_STEP) for pp in range(ATT_PAIRS)]
        for pp in range(ATT_PAIRS):
            buf[pp] = ss[pp]

    def drain(buf, grp):
        for pp in range(ATT_PAIRS):
            update(pp, buf[pp], grp * KEY_STEP, KEY_STEP, False)

    fill(sa_ref, 0)
    attend(blk, 1, causal, True)

    def two_groups(t, carry):
        grp = 2 * t
        fill(sb_ref, grp + 1)
        drain(sa_ref, grp)
        fill(sa_ref, grp + 2)
        drain(sb_ref, grp + 1)
        return carry

    n_trips = (n_full - 1) // 2
    lax.fori_loop(0, n_trips, two_groups, 0)
    grp_left = 2 * n_trips
    n_left = n_full - grp_left

    @pl.when(jnp.logical_and(n_full > 0, n_left == 1))
    def _():
        drain(sa_ref, grp_left)

    @pl.when(jnp.logical_and(n_full > 0, n_left == 2))
    def _():
        fill(sb_ref, grp_left + 1)
        drain(sa_ref, grp_left)
        drain(sb_ref, grp_left + 1)

    done = n_full * KEY_STEP
    width = KEY_STEP // 2
    while width >= 1:
        first = done
        n_piece = width

        @pl.when((blk & width) != 0)
        def _():
            attend(first, n_piece, None, False)

        done = done + (blk & width)
        width //= 2

    for pp in range(ATT_PAIRS):
        inv_l = 1.0 / l_ref[pp]
        o = acc_ref[pp] * inv_l
        o_t = jnp.concatenate([o[:HEAD_DIM, :tq], o[HEAD_DIM:, tq:]], axis=0)
        o_ref[0, :, pp * LANES:(pp + 1) * LANES] = o_t.T.astype(o_ref.dtype)


def _moba_attention(q_aug, k_aug, v_t, *, batch, seq):
    tq = MOBA_BLOCK
    nb = seq // tq
    gw = ATT_PAIRS * LANES
    out = pl.pallas_call(
        _attn_body,
        grid=(batch, ATT_HEADS // (2 * ATT_PAIRS), nb),
        in_specs=[
            pl.BlockSpec((1, ATT_PAIRS, 1, AUG, 2 * tq), lambda b, g, i: (b, g, i, 0, 0)),
            pl.BlockSpec((1, ATT_PAIRS, seq, AUG), lambda b, g, i: (b, g, 0, 0)),
            pl.BlockSpec((1, nb, gw, tq), lambda b, g, i: (b, 0, g, 0)),
        ],
        out_specs=pl.BlockSpec((1, tq, gw), lambda b, g, i: (b, i, g)),
        out_shape=jax.ShapeDtypeStruct((batch, seq, ATT_WIDTH), BF16),
        scratch_shapes=[pltpu.VMEM((ATT_PAIRS, 1, 2 * tq), F32), pltpu.VMEM((ATT_PAIRS, 1, 2 * tq), F32),
                        pltpu.VMEM((ATT_PAIRS, LANES, 2 * tq), F32)]
        + [pltpu.VMEM((ATT_PAIRS, KEY_STEP * tq, 2 * tq), F32)] * 2,
        compiler_params=_params(("parallel", "parallel", "arbitrary")),
        name="moba_attention",
    )(q_aug, k_aug, v_t)
    return out.reshape(batch * seq, ATT_WIDTH)


def _merge_body(x_ref, h_ref, ya_ref, yb_ref, yc_ref, wgate_ref, gb_ref, wbr_ref, wout_ref, out_ref):
    h = h_ref[...]
    d = x_ref.shape[1]
    merged = None
    for n, y_ref in enumerate((ya_ref, yb_ref, yc_ref)):
        gate = _sigmoid(_dot(h, wgate_ref[:, n * d:(n + 1) * d]) + gb_ref[:, n * d:(n + 1) * d])
        term = gate * _dot(y_ref[...], wbr_ref[n])
        merged = term if merged is None else merged + term
    out_ref[...] = x_ref[...] + _dot(merged.astype(BF16), wout_ref[...])


def _merge(x, h, ya, yb, yc, wgate, gb, wbr, wout, *, tm=512):
    t, d = x.shape
    c = CONV_WIDTH
    row_d = pl.BlockSpec((tm, d), lambda i: (i, 0))
    row_c = pl.BlockSpec((tm, c), lambda i: (i, 0))
    return pl.pallas_call(
        _merge_body,
        grid=(t // tm,),
        in_specs=[
            row_d, row_d, row_c, row_c, row_c,
            pl.BlockSpec((d, N_BRANCH * d), lambda i: (0, 0)),
            pl.BlockSpec((1, N_BRANCH * d), lambda i: (0, 0)),
            pl.BlockSpec((N_BRANCH, c, d), lambda i: (0, 0, 0)),
            pl.BlockSpec((d, d), lambda i: (0, 0)),
        ],
        out_specs=row_d,
        out_shape=jax.ShapeDtypeStruct((t, d), F32),
        compiler_params=_params(("parallel",)),
        name="merge",
    )(x, h, ya, yb, yc, wgate, gb, wbr, wout)


def _rope_tables(seq):
    pos = jnp.arange(seq, dtype=F32)
    inv_freq = ROPE_THETA ** (-jnp.arange(0, ROPE_DIM, 2, dtype=F32) / ROPE_DIM)
    ang = pos[:, None] * inv_freq[None, :]
    cos, sin = jnp.cos(ang), jnp.sin(ang)
    half = ROPE_DIM // 2
    ones = jnp.ones((seq, HEAD_DIM - ROPE_DIM), F32)
    zeros = jnp.zeros((seq, HEAD_DIM - ROPE_DIM), F32)
    zh = jnp.zeros((seq, half), F32)
    rc = jnp.concatenate([cos, cos, ones], axis=1)
    rs1 = jnp.concatenate([-sin, zh, zeros], axis=1)
    rs2 = jnp.concatenate([zh, sin, zeros], axis=1)
    reps = LANES // HEAD_DIM
    return tuple(jnp.tile(a, (1, reps)) for a in (rc, rs1, rs2)) + (cos.T, sin.T)


def kernel(x, ffn1_norm, ffn1_wi, ffn1_wo, mix_norm, w_in, conv_w, conv_b, conv_ln_g, conv_ln_b,
           sgu_ln_g, sgu_ln_b, sgu_w, sgu_b, w_branch, gate_b, w_out, ffn2_norm, ffn2_wi, ffn2_wo,
           final_norm):
    batch, seq, d = x.shape
    depth = ffn1_norm.shape[0]
    assert seq % MOBA_BLOCK == 0 and seq % 512 == 0 and d % LANES == 0
    assert KEY_STEP <= seq // MOBA_BLOCK <= MAX_BLOCKS
    xt = x.reshape(batch * seq, d)
    rc, rs1, rs2, cos_t, sin_t = _rope_tables(seq)
    gw = SGU_WIDTH // SGU_GROUPS

    o_sgu = 2 * CONV_WIDTH
    o_q = o_sgu + 2 * SGU_WIDTH
    o_k, o_v, o_gate = o_q + ATT_WIDTH, o_q + 2 * ATT_WIDTH, o_q + 3 * ATT_WIDTH

    def row(a):
        return a.reshape(1, -1)

    for l in range(depth):
        wl = w_in[l]

        def cols(a, b):
            return wl[:, a:b].astype(BF16)

        xt, h = _ffn(xt, row(ffn1_norm[l]), ffn1_wi[l].astype(BF16), ffn1_wo[l].astype(BF16),
                     row(mix_norm[l]), post="both")

        ya = _conv_branch(h, cols(0, CONV_WIDTH), cols(CONV_WIDTH, o_sgu), conv_w[l], row(conv_b[l]),
                          row(conv_ln_g[l]), row(conv_ln_b[l]), seq=seq)

        bs_full = jnp.repeat(sgu_b[l].T, gw, axis=1)
        yb = _sgu_branch(h, cols(o_sgu, o_sgu + SGU_WIDTH), cols(o_sgu + SGU_WIDTH, o_q),
                         row(sgu_ln_g[l]), row(sgu_ln_b[l]), sgu_w[l], bs_full)

        wqv_t = jnp.concatenate([wl[:, o_q:o_k], wl[:, o_v:o_gate]], axis=1).T.astype(BF16)
        q_aug_t, k_aug, v_t = _qkv_route(h, wqv_t, cols(o_k, o_v), rc, rs1, rs2, cos_t, sin_t,
                                         batch=batch, seq=seq)
        yc = _moba_attention(q_aug_t, k_aug, v_t, batch=batch, seq=seq)

        xt = _merge(xt, h, ya, yb, yc, cols(o_gate, wl.shape[1]), row(gate_b[l]),
                    w_branch[l].astype(BF16), w_out[l].astype(BF16))

        last = l == depth - 1
        xt = _ffn(xt, row(ffn2_norm[l]), ffn2_wi[l].astype(BF16), ffn2_wo[l].astype(BF16),
                  row(final_norm), post="norm_only" if last else "none")
    return xt.reshape(batch, seq, d)
```

```python
import functools

import numpy as np
import jax
import jax.numpy as jnp
from jax import lax
from jax.experimental import pallas as pl
from jax.experimental.pallas import tpu as pltpu

D_FF = 2816
CONV_WIDTH = 512
CONV_K = 31
SGU_WIDTH = 512
SGU_GROUPS = 4
SGU_CHUNK = 128
ATT_HEADS = 8
HEAD_DIM = 64
ATT_WIDTH = ATT_HEADS * HEAD_DIM
MOBA_BLOCK = 256
MOBA_TOPK = 3
ROPE_THETA = 500000.0
ROPE_DIM = HEAD_DIM // 4
N_BRANCH = 3
EPS = 1e-6

LANES = 128
SUBLANES = 8
VMEM_LIMIT = 56 * 1024 * 1024
MASK_VALUE = -1e30
CONV_HALO = 32
CONV_ROWS = 32
ATT_PAIRS = 2
KEY_STEP = 4
MAX_BLOCKS = 32
AUG = 2 * LANES

BF16 = jnp.bfloat16
F32 = jnp.float32
NT_DIMS = (((1,), (1,)), ((), ()))


def _dot(a, b):
    return jnp.dot(a, b, preferred_element_type=F32)


def _sigmoid(x):
    return 1.0 / (1.0 + jnp.exp(-x))


def _rms(x, g):
    return x * lax.rsqrt(jnp.mean(x * x, axis=-1, keepdims=True) + EPS) * g


def _params(sem, flags=None):
    return pltpu.CompilerParams(dimension_semantics=sem, vmem_limit_bytes=VMEM_LIMIT, flags=flags)


def _ffn_body(x_ref, g_ref, wg_ref, wu_ref, wo_ref, g2_ref, *refs, post, single_step):
    if post == "both":
        out_ref, nxt_ref = refs[:2]
    else:
        out_ref = refs[0]

    def finish(acc):
        y = x_ref[...] + 0.5 * acc
        if post == "both":
            out_ref[...] = y
            nxt_ref[...] = _rms(y, g2_ref[...]).astype(nxt_ref.dtype)
        elif post == "norm_only":
            out_ref[...] = _rms(y, g2_ref[...])
        else:
            out_ref[...] = y

    def partial_out(xn):
        gate = _dot(xn, wg_ref[...])
        up = _dot(xn, wu_ref[...])
        act = (gate * _sigmoid(gate) * up).astype(BF16)
        return _dot(act, wo_ref[...])

    if single_step:
        finish(partial_out(_rms(x_ref[...], g_ref[...]).astype(BF16)))
        return

    xn_ref, acc_ref = refs[-2:]
    j = pl.program_id(1)

    @pl.when(j == 0)
    def _():
        xn_ref[...] = _rms(x_ref[...], g_ref[...]).astype(BF16)

    part = partial_out(xn_ref[...])

    @pl.when(j == 0)
    def _():
        acc_ref[...] = part

    @pl.when(j > 0)
    def _():
        acc_ref[...] += part

    @pl.when(j == pl.num_programs(1) - 1)
    def _():
        finish(acc_ref[...])


def _ffn(x, g, wi, wo, g2, *, post, tm=512, tf=D_FF):
    t, d = x.shape
    nf = D_FF // tf
    single = nf == 1
    w_mode = dict(pipeline_mode=pl.Buffered(1)) if single else {}
    row = pl.BlockSpec((tm, d), lambda i, j: (i, 0))
    vec = pl.BlockSpec((1, d), lambda i, j: (0, 0))
    in_specs = [
        row, vec,
        pl.BlockSpec((d, tf), lambda i, j: (0, j), **w_mode),
        pl.BlockSpec((d, tf), lambda i, j: (0, j + nf), **w_mode),
        pl.BlockSpec((tf, d), lambda i, j: (j, 0), **w_mode),
        vec,
    ]
    if post == "both":
        out_shape = (jax.ShapeDtypeStruct((t, d), F32), jax.ShapeDtypeStruct((t, d), BF16))
        out_specs = (row, row)
    else:
        out_shape = jax.ShapeDtypeStruct((t, d), F32)
        out_specs = row
    return pl.pallas_call(
        functools.partial(_ffn_body, post=post, single_step=single),
        grid=(t // tm, nf),
        in_specs=in_specs,
        out_specs=out_specs,
        out_shape=out_shape,
        scratch_shapes=[] if single else [pltpu.VMEM((tm, d), BF16), pltpu.VMEM((tm, d), F32)],
        compiler_params=_params(("parallel", "arbitrary")),
        name="ffn",
    )(x, g, wi, wi, wo, g2)


def _conv_body(h_ref, wa_ref, wg_ref, cw_ref, cb_ref, lg_ref, lb_ref, out_ref, buf_ref, conv_ref, *, tm,
               tiles_per_seq):
    t = pl.program_id(0)

    @pl.when(t % tiles_per_seq == 0)
    def _():
        buf_ref[...] = jnp.zeros(buf_ref.shape, F32)

    h = h_ref[...]
    a = _dot(h, wa_ref[...])
    g = _dot(h, wg_ref[...])
    glu = a * _sigmoid(g)
    for r in range(SUBLANES):
        buf_ref[r, CONV_HALO - r:CONV_HALO - r + tm, :] = glu

    first_tap = CONV_HALO - (CONV_K - 1)

    def step(i, carry):
        base = pl.multiple_of(i * CONV_ROWS, CONV_ROWS)
        y = None
        for k in range(CONV_K):
            o = first_tap + k
            term = cw_ref[k:k + 1, :] * buf_ref[o % SUBLANES, pl.ds(base + (o - o % SUBLANES), CONV_ROWS), :]
            y = term if y is None else y + term
        conv_ref[pl.ds(base, CONV_ROWS), :] = y
        return carry

    lax.fori_loop(0, tm // CONV_ROWS, step, 0)
    for r in range(SUBLANES):
        buf_ref[r, 0:CONV_HALO, :] = buf_ref[r, tm:tm + CONV_HALO, :]

    y = conv_ref[...] + cb_ref[...]
    mu = jnp.mean(y, axis=-1, keepdims=True)
    yc = y - mu
    var = jnp.mean(yc * yc, axis=-1, keepdims=True)
    yn = yc * lax.rsqrt(var + EPS) * lg_ref[...] + lb_ref[...]
    out_ref[...] = (yn * _sigmoid(yn)).astype(out_ref.dtype)


def _conv_branch(h, wa, wg, cw, cb, lg, lb, *, seq, tm=512):
    t, d = h.shape
    c = CONV_WIDTH
    vec = pl.BlockSpec((1, c), lambda i: (0, 0))
    return pl.pallas_call(
        functools.partial(_conv_body, tm=tm, tiles_per_seq=seq // tm),
        grid=(t // tm,),
        in_specs=[
            pl.BlockSpec((tm, d), lambda i: (i, 0)),
            pl.BlockSpec((d, c), lambda i: (0, 0)),
            pl.BlockSpec((d, c), lambda i: (0, 0)),
            pl.BlockSpec((CONV_K, c), lambda i: (0, 0)),
            vec, vec, vec,
        ],
        out_specs=pl.BlockSpec((tm, c), lambda i: (i, 0)),
        out_shape=jax.ShapeDtypeStruct((t, c), BF16),
        scratch_shapes=[pltpu.VMEM((SUBLANES, CONV_HALO + tm, c), F32), pltpu.VMEM((tm, c), F32)],
        compiler_params=_params(("arbitrary",)),
        name="conv_branch",
    )(h, wa, wg, cw, cb, lg, lb)


def _gelu_tanh(x):
    c = np.float32(np.sqrt(2.0 / np.pi))
    return x * (0.5 * (1.0 + jnp.tanh(c * (x + 0.044715 * (x * x * x)))))


def _sgu_body(h_ref, wu_ref, wv_ref, lg_ref, lb_ref, ws_ref, bs_ref, out_ref, *, tm):
    h = h_ref[...]
    u = _gelu_tanh(_dot(h, wu_ref[...]))
    v = _gelu_tanh(_dot(h, wv_ref[...]))
    mu = jnp.mean(v, axis=-1, keepdims=True)
    vc = v - mu
    var = jnp.mean(vc * vc, axis=-1, keepdims=True)
    vn = (vc * lax.rsqrt(var + EPS) * lg_ref[...] + lb_ref[...]).astype(BF16)

    n_chunks = tm // SGU_CHUNK
    gw = SGU_WIDTH // SGU_GROUPS
    row = lax.broadcasted_iota(jnp.int32, (SGU_CHUNK, SGU_CHUNK), 0)
    col = lax.broadcasted_iota(jnp.int32, (SGU_CHUNK, SGU_CHUNK), 1)
    bias = bs_ref[...]
    mixed = [[None] * SGU_GROUPS for _ in range(n_chunks)]
    for g in range(SGU_GROUPS):
        w = jnp.where(col <= row, ws_ref[g], 0.0).astype(BF16)
        rhs = jnp.concatenate(
            [vn[n * SGU_CHUNK:(n + 1) * SGU_CHUNK, g * gw:(g + 1) * gw] for n in range(n_chunks)], axis=1)
        res = _dot(w, rhs)
        for n in range(n_chunks):
            mixed[n][g] = res[:, n * gw:(n + 1) * gw]
    for n in range(n_chunks):
        m = jnp.concatenate(mixed[n], axis=1) + bias
        rows = slice(n * SGU_CHUNK, (n + 1) * SGU_CHUNK)
        out_ref[rows, :] = (u[rows, :] * m).astype(out_ref.dtype)


def _sgu_branch(h, wu, wv, lg, lb, ws, bs_full, *, tm=512):
    t, d = h.shape
    c = SGU_WIDTH
    vec = pl.BlockSpec((1, c), lambda i: (0, 0))
    return pl.pallas_call(
        functools.partial(_sgu_body, tm=tm),
        grid=(t // tm,),
        in_specs=[
            pl.BlockSpec((tm, d), lambda i: (i, 0)),
            pl.BlockSpec((d, c), lambda i: (0, 0)),
            pl.BlockSpec((d, c), lambda i: (0, 0)),
            vec, vec,
            pl.BlockSpec((SGU_GROUPS, SGU_CHUNK, SGU_CHUNK), lambda i: (0, 0, 0)),
            pl.BlockSpec((SGU_CHUNK, c), lambda i: (0, 0)),
        ],
        out_specs=pl.BlockSpec((tm, c), lambda i: (i, 0)),
        out_shape=jax.ShapeDtypeStruct((t, c), BF16),
        compiler_params=_params(("parallel",)),
        name="sgu_branch",
    )(h, wu, wv, lg, lb, ws, bs_full)


def _qkv_body(h_ref, wqv_ref, wk_ref, rc_ref, rs1_ref, rs2_ref, ct_ref, st_ref, q_ref, k_ref, v_ref, km_ref):
    blk = pl.program_id(1)

    @pl.when(blk == 0)
    def _():
        km_ref[...] = jnp.zeros(km_ref.shape, F32)

    h = h_ref[...]
    qv = lax.dot_general(wqv_ref[...], h, NT_DIMS, preferred_element_type=F32)
    v_ref[0, 0] = qv[ATT_WIDTH:, :].astype(v_ref.dtype)
    k = _dot(h, wk_ref[...])

    rc, rs1, rs2 = rc_ref[...], rs1_ref[...], rs2_ref[...]
    cos_t, sin_t = ct_ref[...], st_ref[...]
    half = ROPE_DIM // 2
    tokens = h.shape[0]
    scale = HEAD_DIM ** -0.5
    neg_inf = float("-inf")

    lane = lax.broadcasted_iota(jnp.int32, (tokens, LANES), 1)
    km_lane = lax.broadcasted_iota(jnp.int32, (MAX_BLOCKS, LANES), 1)
    blk_row = lax.broadcasted_iota(jnp.int32, (MAX_BLOCKS, tokens), 0)
    blk_row_f = blk_row.astype(F32)
    onehot = jnp.where(lane == blk, 1.0, 0.0)
    zeros_head = jnp.zeros((HEAD_DIM, tokens), F32)
    zeros_pad = jnp.zeros((AUG - 2 * HEAD_DIM - MAX_BLOCKS, 2 * tokens), F32)

    def rope_t(x):
        x1, x2 = x[:half], x[half:ROPE_DIM]
        return jnp.concatenate([x1 * cos_t - x2 * sin_t, x2 * cos_t + x1 * sin_t, x[ROPE_DIM:]], axis=0)

    def route(gate):
        g = jnp.where(blk_row < blk, gate, neg_inf)
        bias = jnp.where(blk_row == blk, 0.0, MASK_VALUE)
        for _ in range(MOBA_TOPK):
            m = jnp.max(g, axis=0, keepdims=True)
            idx = jnp.min(jnp.where(g == m, blk_row_f, 2.0 * MAX_BLOCKS), axis=0, keepdims=True)
            pick = jnp.logical_and(blk_row_f == idx, m > neg_inf)
            bias = jnp.where(pick, 0.0, bias)
            g = jnp.where(pick, neg_inf, g)
        return bias

    for pair in range(ATT_HEADS // 2):
        even, odd = 2 * pair, 2 * pair + 1
        x = k[:, pair * LANES:(pair + 1) * LANES]
        ks = x * rc + pltpu.roll(x, LANES - half, 1) * rs1 + pltpu.roll(x, half, 1) * rs2
        k_ref[0, pair] = jnp.concatenate([ks, onehot], axis=1).astype(k_ref.dtype)

        q_even = rope_t(qv[even * HEAD_DIM:(even + 1) * HEAD_DIM, :]) * scale
        q_odd = rope_t(qv[odd * HEAD_DIM:(odd + 1) * HEAD_DIM, :]) * scale
        q_pair = jnp.concatenate([q_even, q_odd], axis=0)

        km = km_ref[pair]
        gate_even = jnp.dot(jnp.where(km_lane < HEAD_DIM, km, 0.0), q_pair,
                            precision=lax.Precision.HIGHEST, preferred_element_type=F32)
        gate_odd = jnp.dot(jnp.where(km_lane >= HEAD_DIM, km, 0.0), q_pair,
                           precision=lax.Precision.HIGHEST, preferred_element_type=F32)
        q_ref[0, pair, 0] = jnp.concatenate([
            jnp.concatenate([q_even, zeros_head], axis=1),
            jnp.concatenate([zeros_head, q_odd], axis=1),
            jnp.concatenate([route(gate_even), route(gate_odd)], axis=1),
            zeros_pad], axis=0).astype(q_ref.dtype)

        km_ref[pair, pl.ds(blk, 1), :] = jnp.sum(ks, axis=0, keepdims=True) * (1.0 / MOBA_BLOCK)


def _qkv_route(h, wqv_t, wk, rc, rs1, rs2, cos_t, sin_t, *, batch, seq):
    t, d = h.shape
    tm = MOBA_BLOCK
    nb = seq // tm
    pairs = ATT_HEADS // 2
    r_spec = pl.BlockSpec((tm, LANES), lambda b, i: (i, 0))
    t_spec = pl.BlockSpec((ROPE_DIM // 2, tm), lambda b, i: (0, i))
    return pl.pallas_call(
        _qkv_body,
        grid=(batch, nb),
        in_specs=[pl.BlockSpec((tm, d), lambda b, i: (b * nb + i, 0)),
                  pl.BlockSpec((2 * ATT_WIDTH, d), lambda b, i: (0, 0)),
                  pl.BlockSpec((d, ATT_WIDTH), lambda b, i: (0, 0)),
                  r_spec, r_spec, r_spec, t_spec, t_spec],
        out_specs=(pl.BlockSpec((1, pairs, 1, AUG, 2 * tm), lambda b, i: (b, 0, i, 0, 0)),
                   pl.BlockSpec((1, pairs, tm, AUG), lambda b, i: (b, 0, i, 0)),
                   pl.BlockSpec((1, 1, ATT_WIDTH, tm), lambda b, i: (b, i, 0, 0))),
        out_shape=(jax.ShapeDtypeStruct((batch, pairs, nb, AUG, 2 * tm), BF16),
                   jax.ShapeDtypeStruct((batch, pairs, seq, AUG), BF16),
                   jax.ShapeDtypeStruct((batch, nb, ATT_WIDTH, tm), BF16)),
        scratch_shapes=[pltpu.VMEM((ATT_HEADS // 2, MAX_BLOCKS, LANES), F32)],
        compiler_params=_params(("arbitrary", "arbitrary")),
        name="qkv_route",
    )(h, wqv_t, wk, rc, rs1, rs2, cos_t, sin_t)


def _attn_body(q_ref, k_ref, v_ref, o_ref, m_ref, l_ref, acc_ref, sa_ref, sb_ref, ma_ref, mb_ref):
    blk = pl.program_id(2)
    tq = MOBA_BLOCK
    key = lax.broadcasted_iota(jnp.int32, (tq, 2 * tq), 0)
    lane = lax.broadcasted_iota(jnp.int32, (tq, 2 * tq), 1)
    qry = jnp.where(lane < tq, lane, lane - tq)
    causal = jnp.where(key <= qry, 0.0, MASK_VALUE)

    def scores(pp, first_blk, n_blk):
        start = pl.multiple_of(first_blk * tq, tq)
        return _dot(k_ref[0, pp, pl.ds(start, n_blk * tq), :], q_ref[0, pp, 0])

    def update(pp, s, first_blk, n_blk, init, m_cur=None):
        v_t = jnp.concatenate([v_ref[0, first_blk + a, pp * LANES:(pp + 1) * LANES, :]
                               for a in range(n_blk)], axis=1)
        if m_cur is None:
            m_cur = jnp.max(s, axis=0, keepdims=True)
        if init:
            p = jnp.exp(s - m_cur)
            m_ref[pp] = m_cur
            l_ref[pp] = jnp.sum(p, axis=0, keepdims=True)
            acc_ref[pp] = _dot(v_t, p.astype(BF16))
        else:
            m_old = m_ref[pp]
            m_new = jnp.maximum(m_old, m_cur)
            alpha = jnp.exp(m_old - m_new)
            p = jnp.exp(s - m_new)
            l_ref[pp] = alpha * l_ref[pp] + jnp.sum(p, axis=0, keepdims=True)
            m_ref[pp] = m_new
            acc_ref[pp] = alpha * acc_ref[pp] + _dot(v_t, p.astype(BF16))

    n_full = blk // KEY_STEP

    def fill(bufs, grp):
        buf, mbuf = bufs
        ss = [scores(pp, grp * KEY_STEP, KEY_STEP) for pp in range(ATT_PAIRS)]
        for pp in range(ATT_PAIRS):
            buf[pp] = ss[pp]
            mbuf[pp] = jnp.max(ss[pp], axis=0, keepdims=True)

    def drain(bufs, grp):
        buf, mbuf = bufs
        for pp in range(ATT_PAIRS):
            update(pp, buf[pp], grp * KEY_STEP, KEY_STEP, False, mbuf[pp])

    sa_ref, sb_ref = (sa_ref, ma_ref), (sb_ref, mb_ref)

    n_rem = blk - n_full * KEY_STEP
    for rem in range(KEY_STEP):
        @pl.when(n_rem == rem)
        def _():
            fill(sa_ref, 0)
            ss = [scores(pp, blk - rem, rem + 1) for pp in range(ATT_PAIRS)]
            mask = causal if rem == 0 else jnp.concatenate(
                [jnp.zeros((rem * tq, 2 * tq), F32), causal], axis=0)
            for pp in range(ATT_PAIRS):
                update(pp, ss[pp] + mask, blk - rem, rem + 1, True)

    def two_groups(t, carry):
        grp = 2 * t
        fill(sb_ref, grp + 1)
        drain(sa_ref, grp)
        fill(sa_ref, grp + 2)
        drain(sb_ref, grp + 1)
        return carry

    n_trips = (n_full - 1) // 2
    lax.fori_loop(0, n_trips, two_groups, 0)
    grp_left = 2 * n_trips
    n_left = n_full - grp_left

    @pl.when(jnp.logical_and(n_full > 0, n_left == 1))
    def _():
        drain(sa_ref, grp_left)

    @pl.when(jnp.logical_and(n_full > 0, n_left == 2))
    def _():
        fill(sb_ref, grp_left + 1)
        drain(sa_ref, grp_left)
        drain(sb_ref, grp_left + 1)

    for pp in range(ATT_PAIRS):
        inv_l = 1.0 / l_ref[pp]
        o = acc_ref[pp] * inv_l
        o_t = jnp.concatenate([o[:HEAD_DIM, :tq], o[HEAD_DIM:, tq:]], axis=0)
        o_ref[0, :, pp * LANES:(pp + 1) * LANES] = o_t.T.astype(o_ref.dtype)


def _moba_attention(q_aug, k_aug, v_t, *, batch, seq):
    tq = MOBA_BLOCK
    nb = seq // tq
    gw = ATT_PAIRS * LANES
    out = pl.pallas_call(
        _attn_body,
        grid=(batch, ATT_HEADS // (2 * ATT_PAIRS), nb),
        in_specs=[
            pl.BlockSpec((1, ATT_PAIRS, 1, AUG, 2 * tq), lambda b, g, i: (b, g, i, 0, 0)),
            pl.BlockSpec((1, ATT_PAIRS, seq, AUG), lambda b, g, i: (b, g, 0, 0)),
            pl.BlockSpec((1, nb, gw, tq), lambda b, g, i: (b, 0, g, 0)),
        ],
        out_specs=pl.BlockSpec((1, tq, gw), lambda b, g, i: (b, i, g)),
        out_shape=jax.ShapeDtypeStruct((batch, seq, ATT_WIDTH), BF16),
        scratch_shapes=[pltpu.VMEM((ATT_PAIRS, 1, 2 * tq), F32), pltpu.VMEM((ATT_PAIRS, 1, 2 * tq), F32),
                        pltpu.VMEM((ATT_PAIRS, LANES, 2 * tq), F32)]
        + [pltpu.VMEM((ATT_PAIRS, KEY_STEP * tq, 2 * tq), F32)] * 2
        + [pltpu.VMEM((ATT_PAIRS, 1, 2 * tq), F32)] * 2,
        compiler_params=_params(("parallel", "parallel", "arbitrary")),
        name="moba_attention",
    )(q_aug, k_aug, v_t)
    return out.reshape(batch * seq, ATT_WIDTH)


def _merge_body(x_ref, h_ref, ya_ref, yb_ref, yc_ref, wgate_ref, gb_ref, wbr_ref, wout_ref, out_ref):
    h = h_ref[...]
    d = x_ref.shape[1]
    merged = None
    for n, y_ref in enumerate((ya_ref, yb_ref, yc_ref)):
        gate = _sigmoid(_dot(h, wgate_ref[:, n * d:(n + 1) * d]) + gb_ref[:, n * d:(n + 1) * d])
        term = gate * _dot(y_ref[...], wbr_ref[n])
        merged = term if merged is None else merged + term
    out_ref[...] = x_ref[...] + _dot(merged.astype(BF16), wout_ref[...])


def _merge(x, h, ya, yb, yc, wgate, gb, wbr, wout, *, tm=512):
    t, d = x.shape
    c = CONV_WIDTH
    row_d = pl.BlockSpec((tm, d), lambda i: (i, 0))
    row_c = pl.BlockSpec((tm, c), lambda i: (i, 0))
    return pl.pallas_call(
        _merge_body,
        grid=(t // tm,),
        in_specs=[
            row_d, row_d, row_c, row_c, row_c,
            pl.BlockSpec((d, N_BRANCH * d), lambda i: (0, 0)),
            pl.BlockSpec((1, N_BRANCH * d), lambda i: (0, 0)),
            pl.BlockSpec((N_BRANCH, c, d), lambda i: (0, 0, 0)),
            pl.BlockSpec((d, d), lambda i: (0, 0)),
        ],
        out_specs=row_d,
        out_shape=jax.ShapeDtypeStruct((t, d), F32),
        compiler_params=_params(("parallel",)),
        name="merge",
    )(x, h, ya, yb, yc, wgate, gb, wbr, wout)


def _rope_tables(seq):
    pos = jnp.arange(seq, dtype=F32)
    inv_freq = ROPE_THETA ** (-jnp.arange(0, ROPE_DIM, 2, dtype=F32) / ROPE_DIM)
    ang = pos[:, None] * inv_freq[None, :]
    cos, sin = jnp.cos(ang), jnp.sin(ang)
    half = ROPE_DIM // 2
    ones = jnp.ones((seq, HEAD_DIM - ROPE_DIM), F32)
    zeros = jnp.zeros((seq, HEAD_DIM - ROPE_DIM), F32)
    zh = jnp.zeros((seq, half), F32)
    rc = jnp.concatenate([cos, cos, ones], axis=1)
    rs1 = jnp.concatenate([-sin, zh, zeros], axis=1)
    rs2 = jnp.concatenate([zh, sin, zeros], axis=1)
    reps = LANES // HEAD_DIM
    return tuple(jnp.tile(a, (1, reps)) for a in (rc, rs1, rs2)) + (cos.T, sin.T)


def kernel(x, ffn1_norm, ffn1_wi, ffn1_wo, mix_norm, w_in, conv_w, conv_b, conv_ln_g, conv_ln_b,
           sgu_ln_g, sgu_ln_b, sgu_w, sgu_b, w_branch, gate_b, w_out, ffn2_norm, ffn2_wi, ffn2_wo,
           final_norm):
    batch, seq, d = x.shape
    depth = ffn1_norm.shape[0]
    assert seq % MOBA_BLOCK == 0 and seq % 512 == 0 and d % LANES == 0
    assert KEY_STEP <= seq // MOBA_BLOCK <= MAX_BLOCKS
    xt = x.reshape(batch * seq, d)
    rc, rs1, rs2, cos_t, sin_t = _rope_tables(seq)
    gw = SGU_WIDTH // SGU_GROUPS

    o_sgu = 2 * CONV_WIDTH
    o_q = o_sgu + 2 * SGU_WIDTH
    o_k, o_v, o_gate = o_q + ATT_WIDTH, o_q + 2 * ATT_WIDTH, o_q + 3 * ATT_WIDTH

    def row(a):
        return a.reshape(1, -1)

    for l in range(depth):
        wl = w_in[l]

        def cols(a, b):
            return wl[:, a:b].astype(BF16)

        xt, h = _ffn(xt, row(ffn1_norm[l]), ffn1_wi[l].astype(BF16), ffn1_wo[l].astype(BF16),
                     row(mix_norm[l]), post="both")

        ya = _conv_branch(h, cols(0, CONV_WIDTH), cols(CONV_WIDTH, o_sgu), conv_w[l], row(conv_b[l]),
                          row(conv_ln_g[l]), row(conv_ln_b[l]), seq=seq)

        bs_full = jnp.repeat(sgu_b[l].T, gw, axis=1)
        yb = _sgu_branch(h, cols(o_sgu, o_sgu + SGU_WIDTH), cols(o_sgu + SGU_WIDTH, o_q),
                         row(sgu_ln_g[l]), row(sgu_ln_b[l]), sgu_w[l], bs_full)

        wqv_t = jnp.concatenate([wl[:, o_q:o_k], wl[:, o_v:o_gate]], axis=1).T.astype(BF16)
        q_aug_t, k_aug, v_t = _qkv_route(h, wqv_t, cols(o_k, o_v), rc, rs1, rs2, cos_t, sin_t,
                                         batch=batch, seq=seq)
        yc = _moba_attention(q_aug_t, k_aug, v_t, batch=batch, seq=seq)

        xt = _merge(xt, h, ya, yb, yc, cols(o_gate, wl.shape[1]), row(gate_b[l]),
                    w_branch[l].astype(BF16), w_out[l].astype(BF16))

        last = l == depth - 1
        xt = _ffn(xt, row(ffn2_norm[l]), ffn2_wi[l].astype(BF16), ffn2_wo[l].astype(BF16),
                  row(final_norm), post="norm_only" if last else "none")
    return xt.reshape(batch, seq, d)
```

```python
import functools

import numpy as np
import jax
import jax.numpy as jnp
from jax import lax
from jax.experimental import pallas as pl
from jax.experimental.pallas import tpu as pltpu

D_FF = 2816
CONV_WIDTH = 512
CONV_K = 31
SGU_WIDTH = 512
SGU_GROUPS = 4
SGU_CHUNK = 128
ATT_HEADS = 8
HEAD_DIM = 64
ATT_WIDTH = ATT_HEADS * HEAD_DIM
MOBA_BLOCK = 256
MOBA_TOPK = 3
ROPE_THETA = 500000.0
ROPE_DIM = HEAD_DIM // 4
N_BRANCH = 3
EPS = 1e-6

LANES = 128
SUBLANES = 8
VMEM_LIMIT = 56 * 1024 * 1024
MASK_VALUE = -1e30
CONV_HALO = 32
CONV_ROWS = 32
QKV_BLOCKS = 2
ATT_PAIRS = 4
KEY_STEP = 4
MAX_BLOCKS = 32
AUG = 2 * LANES

BF16 = jnp.bfloat16
F32 = jnp.float32
NT_DIMS = (((1,), (1,)), ((), ()))


def _dot(a, b):
    return jnp.dot(a, b, preferred_element_type=F32)


def _sigmoid(x):
    return 1.0 / (1.0 + jnp.exp(-x))


def _rms(x, g):
    return x * lax.rsqrt(jnp.mean(x * x, axis=-1, keepdims=True) + EPS) * g


def _params(sem, flags=None):
    return pltpu.CompilerParams(dimension_semantics=sem, vmem_limit_bytes=VMEM_LIMIT, flags=flags)


def _ffn_body(x_ref, g_ref, wg_ref, wu_ref, wo_ref, g2_ref, *refs, post, single_step):
    if post == "both":
        out_ref, nxt_ref = refs[:2]
    else:
        out_ref = refs[0]

    def finish(acc):
        y = x_ref[...] + 0.5 * acc
        if post == "both":
            out_ref[...] = y
            nxt_ref[...] = _rms(y, g2_ref[...]).astype(nxt_ref.dtype)
        elif post == "norm_only":
            out_ref[...] = _rms(y, g2_ref[...])
        else:
            out_ref[...] = y

    def partial_out(xn):
        gate = _dot(xn, wg_ref[...])
        up = _dot(xn, wu_ref[...])
        act = (gate * _sigmoid(gate) * up).astype(BF16)
        return _dot(act, wo_ref[...])

    if single_step:
        finish(partial_out(_rms(x_ref[...], g_ref[...]).astype(BF16)))
        return

    xn_ref, acc_ref = refs[-2:]
    j = pl.program_id(1)

    @pl.when(j == 0)
    def _():
        xn_ref[...] = _rms(x_ref[...], g_ref[...]).astype(BF16)

    part = partial_out(xn_ref[...])

    @pl.when(j == 0)
    def _():
        acc_ref[...] = part

    @pl.when(j > 0)
    def _():
        acc_ref[...] += part

    @pl.when(j == pl.num_programs(1) - 1)
    def _():
        finish(acc_ref[...])


def _ffn(x, g, wi, wo, g2, *, post, tm=512, tf=D_FF):
    t, d = x.shape
    nf = D_FF // tf
    single = nf == 1
    w_mode = dict(pipeline_mode=pl.Buffered(1)) if single else {}
    row = pl.BlockSpec((tm, d), lambda i, j: (i, 0))
    vec = pl.BlockSpec((1, d), lambda i, j: (0, 0))
    in_specs = [
        row, vec,
        pl.BlockSpec((d, tf), lambda i, j: (0, j), **w_mode),
        pl.BlockSpec((d, tf), lambda i, j: (0, j + nf), **w_mode),
        pl.BlockSpec((tf, d), lambda i, j: (j, 0), **w_mode),
        vec,
    ]
    if post == "both":
        out_shape = (jax.ShapeDtypeStruct((t, d), F32), jax.ShapeDtypeStruct((t, d), BF16))
        out_specs = (row, row)
    else:
        out_shape = jax.ShapeDtypeStruct((t, d), F32)
        out_specs = row
    return pl.pallas_call(
        functools.partial(_ffn_body, post=post, single_step=single),
        grid=(t // tm, nf),
        in_specs=in_specs,
        out_specs=out_specs,
        out_shape=out_shape,
        scratch_shapes=[] if single else [pltpu.VMEM((tm, d), BF16), pltpu.VMEM((tm, d), F32)],
        compiler_params=_params(("parallel", "arbitrary")),
        name="ffn",
    )(x, g, wi, wi, wo, g2)


def _conv_body(h_ref, wa_ref, wg_ref, cw_ref, cb_ref, lg_ref, lb_ref, out_ref, buf_ref, conv_ref, *, tm,
               tiles_per_seq):
    t = pl.program_id(0)

    @pl.when(t % tiles_per_seq == 0)
    def _():
        buf_ref[...] = jnp.zeros(buf_ref.shape, F32)

    h = h_ref[...]
    a = _dot(h, wa_ref[...])
    g = _dot(h, wg_ref[...])
    glu = a * _sigmoid(g)
    for r in range(SUBLANES):
        buf_ref[r, CONV_HALO - r:CONV_HALO - r + tm, :] = glu

    first_tap = CONV_HALO - (CONV_K - 1)

    def step(i, carry):
        base = pl.multiple_of(i * CONV_ROWS, CONV_ROWS)
        y = None
        for k in range(CONV_K):
            o = first_tap + k
            win = buf_ref[o % SUBLANES, pl.ds(base + (o - o % SUBLANES), CONV_ROWS), :]
            term = win.reshape(CONV_ROWS // SUBLANES, SUBLANES, CONV_WIDTH) * cw_ref[k]
            y = term if y is None else y + term
        conv_ref[pl.ds(base, CONV_ROWS), :] = y.reshape(CONV_ROWS, CONV_WIDTH)
        return carry

    lax.fori_loop(0, tm // CONV_ROWS, step, 0)
    for r in range(SUBLANES):
        buf_ref[r, 0:CONV_HALO, :] = buf_ref[r, tm:tm + CONV_HALO, :]

    y = conv_ref[...] + cb_ref[...]
    mu = jnp.mean(y, axis=-1, keepdims=True)
    yc = y - mu
    var = jnp.mean(yc * yc, axis=-1, keepdims=True)
    yn = yc * lax.rsqrt(var + EPS) * lg_ref[...] + lb_ref[...]
    out_ref[...] = (yn * _sigmoid(yn)).astype(out_ref.dtype)


def _conv_branch(h, wa, wg, cw, cb, lg, lb, *, seq, tm=512):
    t, d = h.shape
    c = CONV_WIDTH
    vec = pl.BlockSpec((1, c), lambda i: (0, 0))
    return pl.pallas_call(
        functools.partial(_conv_body, tm=tm, tiles_per_seq=seq // tm),
        grid=(t // tm,),
        in_specs=[
            pl.BlockSpec((tm, d), lambda i: (i, 0)),
            pl.BlockSpec((d, c), lambda i: (0, 0)),
            pl.BlockSpec((d, c), lambda i: (0, 0)),
            pl.BlockSpec((CONV_K, SUBLANES, c), lambda i: (0, 0, 0)),
            vec, vec, vec,
        ],
        out_specs=pl.BlockSpec((tm, c), lambda i: (i, 0)),
        out_shape=jax.ShapeDtypeStruct((t, c), BF16),
        scratch_shapes=[pltpu.VMEM((SUBLANES, CONV_HALO + tm, c), F32), pltpu.VMEM((tm, c), F32)],
        compiler_params=_params(("arbitrary",)),
        name="conv_branch",
    )(h, wa, wg, cw, cb, lg, lb)


def _gelu_tanh(x):
    c = np.float32(np.sqrt(2.0 / np.pi))
    return x * (0.5 * (1.0 + jnp.tanh(c * (x + 0.044715 * (x * x * x)))))


def _sgu_body(h_ref, wu_ref, wv_ref, lg_ref, lb_ref, ws_ref, bs_ref, out_ref, *, tm):
    h = h_ref[...]
    u = _gelu_tanh(_dot(h, wu_ref[...]))
    v = _gelu_tanh(_dot(h, wv_ref[...]))
    mu = jnp.mean(v, axis=-1, keepdims=True)
    vc = v - mu
    var = jnp.mean(vc * vc, axis=-1, keepdims=True)
    vn = (vc * lax.rsqrt(var + EPS) * lg_ref[...] + lb_ref[...]).astype(BF16)

    n_chunks = tm // SGU_CHUNK
    gw = SGU_WIDTH // SGU_GROUPS
    row = lax.broadcasted_iota(jnp.int32, (SGU_CHUNK, SGU_CHUNK), 0)
    col = lax.broadcasted_iota(jnp.int32, (SGU_CHUNK, SGU_CHUNK), 1)
    bias = bs_ref[...]
    mixed = [[None] * SGU_GROUPS for _ in range(n_chunks)]
    for g in range(SGU_GROUPS):
        w = jnp.where(col <= row, ws_ref[g], 0.0).astype(BF16)
        rhs = jnp.concatenate(
            [vn[n * SGU_CHUNK:(n + 1) * SGU_CHUNK, g * gw:(g + 1) * gw] for n in range(n_chunks)], axis=1)
        res = _dot(w, rhs)
        for n in range(n_chunks):
            mixed[n][g] = res[:, n * gw:(n + 1) * gw]
    for n in range(n_chunks):
        m = jnp.concatenate(mixed[n], axis=1) + bias
        rows = slice(n * SGU_CHUNK, (n + 1) * SGU_CHUNK)
        out_ref[rows, :] = (u[rows, :] * m).astype(out_ref.dtype)


def _sgu_branch(h, wu, wv, lg, lb, ws, bs_full, *, tm=512):
    t, d = h.shape
    c = SGU_WIDTH
    vec = pl.BlockSpec((1, c), lambda i: (0, 0))
    return pl.pallas_call(
        functools.partial(_sgu_body, tm=tm),
        grid=(t // tm,),
        in_specs=[
            pl.BlockSpec((tm, d), lambda i: (i, 0)),
            pl.BlockSpec((d, c), lambda i: (0, 0)),
            pl.BlockSpec((d, c), lambda i: (0, 0)),
            vec, vec,
            pl.BlockSpec((SGU_GROUPS, SGU_CHUNK, SGU_CHUNK), lambda i: (0, 0, 0)),
            pl.BlockSpec((SGU_CHUNK, c), lambda i: (0, 0)),
        ],
        out_specs=pl.BlockSpec((tm, c), lambda i: (i, 0)),
        out_shape=jax.ShapeDtypeStruct((t, c), BF16),
        compiler_params=_params(("parallel",)),
        name="sgu_branch",
    )(h, wu, wv, lg, lb, ws, bs_full)


def _qkv_body(h_ref, wqv_ref, wk_ref, rc_ref, rs1_ref, rs2_ref, ct_ref, st_ref, q_ref, k_ref, v_ref, km_ref):
    step = pl.program_id(1)

    @pl.when(step == 0)
    def _():
        km_ref[...] = jnp.zeros(km_ref.shape, F32)

    h = h_ref[...]
    qv_all = lax.dot_general(wqv_ref[...], h, NT_DIMS, preferred_element_type=F32)
    k_all = _dot(h, wk_ref[...])

    half = ROPE_DIM // 2
    tokens = MOBA_BLOCK
    scale = HEAD_DIM ** -0.5
    neg_inf = float("-inf")

    lane = lax.broadcasted_iota(jnp.int32, (tokens, LANES), 1)
    km_lane = lax.broadcasted_iota(jnp.int32, (MAX_BLOCKS, LANES), 1)
    blk_row = lax.broadcasted_iota(jnp.int32, (MAX_BLOCKS, tokens), 0)
    blk_row_f = blk_row.astype(F32)
    zeros_head = jnp.zeros((HEAD_DIM, tokens), F32)
    zeros_pad = jnp.zeros((AUG - 2 * HEAD_DIM - MAX_BLOCKS, 2 * tokens), F32)

    for sub in range(QKV_BLOCKS):
        blk = step * QKV_BLOCKS + sub
        rows = slice(sub * tokens, (sub + 1) * tokens)
        qv = qv_all[:, rows]
        k = k_all[rows, :]
        v_ref[0, sub] = qv[ATT_WIDTH:, :].astype(v_ref.dtype)
        rc, rs1, rs2 = rc_ref[rows, :], rs1_ref[rows, :], rs2_ref[rows, :]
        cos_t, sin_t = ct_ref[:, rows], st_ref[:, rows]
        onehot = jnp.where(lane == blk, 1.0, 0.0)

        def rope_t(x):
            x1, x2 = x[:half], x[half:ROPE_DIM]
            return jnp.concatenate([x1 * cos_t - x2 * sin_t, x2 * cos_t + x1 * sin_t, x[ROPE_DIM:]], axis=0)

        def route(gate):
            g = jnp.where(blk_row < blk, gate, neg_inf)
            bias = jnp.where(blk_row == blk, 0.0, MASK_VALUE)
            for _ in range(MOBA_TOPK):
                m = jnp.max(g, axis=0, keepdims=True)
                idx = jnp.min(jnp.where(g == m, blk_row_f, 2.0 * MAX_BLOCKS), axis=0, keepdims=True)
                pick = jnp.logical_and(blk_row_f == idx, m > neg_inf)
                bias = jnp.where(pick, 0.0, bias)
                g = jnp.where(pick, neg_inf, g)
            return bias

        for pair in range(ATT_HEADS // 2):
            even, odd = 2 * pair, 2 * pair + 1
            x = k[:, pair * LANES:(pair + 1) * LANES]
            ks = x * rc + pltpu.roll(x, LANES - half, 1) * rs1 + pltpu.roll(x, half, 1) * rs2
            k_ref[0, pair, rows, :] = jnp.concatenate([ks, onehot], axis=1).astype(k_ref.dtype)

            q_even = rope_t(qv[even * HEAD_DIM:(even + 1) * HEAD_DIM, :]) * scale
            q_odd = rope_t(qv[odd * HEAD_DIM:(odd + 1) * HEAD_DIM, :]) * scale
            q_pair = jnp.concatenate([q_even, q_odd], axis=0)

            km = km_ref[pair]
            gate_even = jnp.dot(jnp.where(km_lane < HEAD_DIM, km, 0.0), q_pair,
                                precision=lax.Precision.HIGHEST, preferred_element_type=F32)
            gate_odd = jnp.dot(jnp.where(km_lane >= HEAD_DIM, km, 0.0), q_pair,
                               precision=lax.Precision.HIGHEST, preferred_element_type=F32)
            q_ref[0, pair, sub] = jnp.concatenate([
                jnp.concatenate([q_even, zeros_head], axis=1),
                jnp.concatenate([zeros_head, q_odd], axis=1),
                jnp.concatenate([route(gate_even), route(gate_odd)], axis=1),
                zeros_pad], axis=0).astype(q_ref.dtype)

            km_ref[pair, pl.ds(blk, 1), :] = jnp.sum(ks, axis=0, keepdims=True) * (1.0 / MOBA_BLOCK)


def _qkv_route(h, wqv_t, wk, rc, rs1, rs2, cos_t, sin_t, *, batch, seq):
    t, d = h.shape
    blk_tokens = MOBA_BLOCK
    tm = QKV_BLOCKS * blk_tokens
    nb = seq // blk_tokens
    ns = seq // tm
    pairs = ATT_HEADS // 2
    r_spec = pl.BlockSpec((tm, LANES), lambda b, i: (i, 0))
    t_spec = pl.BlockSpec((ROPE_DIM // 2, tm), lambda b, i: (0, i))
    return pl.pallas_call(
        _qkv_body,
        grid=(batch, ns),
        in_specs=[pl.BlockSpec((tm, d), lambda b, i: (b * ns + i, 0)),
                  pl.BlockSpec((2 * ATT_WIDTH, d), lambda b, i: (0, 0)),
                  pl.BlockSpec((d, ATT_WIDTH), lambda b, i: (0, 0)),
                  r_spec, r_spec, r_spec, t_spec, t_spec],
        out_specs=(pl.BlockSpec((1, pairs, QKV_BLOCKS, AUG, 2 * blk_tokens), lambda b, i: (b, 0, i, 0, 0)),
                   pl.BlockSpec((1, pairs, tm, AUG), lambda b, i: (b, 0, i, 0)),
                   pl.BlockSpec((1, QKV_BLOCKS, ATT_WIDTH, blk_tokens), lambda b, i: (b, i, 0, 0))),
        out_shape=(jax.ShapeDtypeStruct((batch, pairs, nb, AUG, 2 * blk_tokens), BF16),
                   jax.ShapeDtypeStruct((batch, pairs, seq, AUG), BF16),
                   jax.ShapeDtypeStruct((batch, nb, ATT_WIDTH, blk_tokens), BF16)),
        scratch_shapes=[pltpu.VMEM((ATT_HEADS // 2, MAX_BLOCKS, LANES), F32)],
        compiler_params=_params(("arbitrary", "arbitrary")),
        name="qkv_route",
    )(h, wqv_t, wk, rc, rs1, rs2, cos_t, sin_t)


def _attn_body(q_ref, k_ref, v_ref, o_ref, m_ref, l_ref, acc_ref, sa_ref, sb_ref, ma_ref, mb_ref):
    blk = pl.program_id(2)
    tq = MOBA_BLOCK
    key = lax.broadcasted_iota(jnp.int32, (tq, 2 * tq), 0)
    lane = lax.broadcasted_iota(jnp.int32, (tq, 2 * tq), 1)
    qry = jnp.where(lane < tq, lane, lane - tq)
    causal = jnp.where(key <= qry, 0.0, MASK_VALUE)

    def scores(pp, first_blk, n_blk):
        start = pl.multiple_of(first_blk * tq, tq)
        return _dot(k_ref[0, pp, pl.ds(start, n_blk * tq), :], q_ref[0, pp, 0])

    def update(pp, s, first_blk, n_blk, init, m_cur=None):
        v_t = jnp.concatenate([v_ref[0, first_blk + a, pp * LANES:(pp + 1) * LANES, :]
                               for a in range(n_blk)], axis=1)
        if m_cur is None:
            m_cur = jnp.max(s, axis=0, keepdims=True)
        if init:
            p = jnp.exp(s - m_cur)
            m_ref[pp] = m_cur
            l_ref[pp] = jnp.sum(p, axis=0, keepdims=True)
            acc_ref[pp] = _dot(v_t, p.astype(BF16))
        else:
            m_old = m_ref[pp]
            m_new = jnp.maximum(m_old, m_cur)
            alpha = jnp.exp(m_old - m_new)
            p = jnp.exp(s - m_new)
            l_ref[pp] = alpha * l_ref[pp] + jnp.sum(p, axis=0, keepdims=True)
            m_ref[pp] = m_new
            acc_ref[pp] = alpha * acc_ref[pp] + _dot(v_t, p.astype(BF16))

    n_full = blk // KEY_STEP

    def fill(bufs, grp):
        buf, mbuf = bufs
        ss = [scores(pp, grp * KEY_STEP, KEY_STEP) for pp in range(ATT_PAIRS)]
        for pp in range(ATT_PAIRS):
            buf[pp] = ss[pp]
            mbuf[pp] = jnp.max(ss[pp], axis=0, keepdims=True)

    def drain(bufs, grp):
        buf, mbuf = bufs
        for pp in range(ATT_PAIRS):
            update(pp, buf[pp], grp * KEY_STEP, KEY_STEP, False, mbuf[pp])

    sa_ref, sb_ref = (sa_ref, ma_ref), (sb_ref, mb_ref)

    n_rem = blk - n_full * KEY_STEP
    for rem in range(KEY_STEP):
        @pl.when(n_rem == rem)
        def _():
            fill(sa_ref, 0)
            ss = [scores(pp, blk - rem, rem + 1) for pp in range(ATT_PAIRS)]
            mask = causal if rem == 0 else jnp.concatenate(
                [jnp.zeros((rem * tq, 2 * tq), F32), causal], axis=0)
            for pp in range(ATT_PAIRS):
                update(pp, ss[pp] + mask, blk - rem, rem + 1, True)

    def two_groups(t, carry):
        grp = 2 * t
        fill(sb_ref, grp + 1)
        drain(sa_ref, grp)
        fill(sa_ref, grp + 2)
        drain(sb_ref, grp + 1)
        return carry

    n_trips = (n_full - 1) // 2
    lax.fori_loop(0, n_trips, two_groups, 0)
    grp_left = 2 * n_trips
    n_left = n_full - grp_left

    @pl.when(jnp.logical_and(n_full > 0, n_left == 1))
    def _():
        drain(sa_ref, grp_left)

    @pl.when(jnp.logical_and(n_full > 0, n_left == 2))
    def _():
        fill(sb_ref, grp_left + 1)
        drain(sa_ref, grp_left)
        drain(sb_ref, grp_left + 1)

    for pp in range(ATT_PAIRS):
        inv_l = 1.0 / l_ref[pp]
        o = acc_ref[pp] * inv_l
        o_t = jnp.concatenate([o[:HEAD_DIM, :tq], o[HEAD_DIM:, tq:]], axis=0)
        o_ref[0, :, pp * LANES:(pp + 1) * LANES] = o_t.T.astype(o_ref.dtype)


def _moba_attention(q_aug, k_aug, v_t, *, batch, seq):
    tq = MOBA_BLOCK
    nb = seq // tq
    gw = ATT_PAIRS * LANES
    out = pl.pallas_call(
        _attn_body,
        grid=(batch, ATT_HEADS // (2 * ATT_PAIRS), nb),
        in_specs=[
            pl.BlockSpec((1, ATT_PAIRS, 1, AUG, 2 * tq), lambda b, g, i: (b, g, i, 0, 0)),
            pl.BlockSpec((1, ATT_PAIRS, seq, AUG), lambda b, g, i: (b, g, 0, 0), pipeline_mode=pl.Buffered(1)),
            pl.BlockSpec((1, nb, gw, tq), lambda b, g, i: (b, 0, g, 0), pipeline_mode=pl.Buffered(1)),
        ],
        out_specs=pl.BlockSpec((1, tq, gw), lambda b, g, i: (b, i, g)),
        out_shape=jax.ShapeDtypeStruct((batch, seq, ATT_WIDTH), BF16),
        scratch_shapes=[pltpu.VMEM((ATT_PAIRS, 1, 2 * tq), F32), pltpu.VMEM((ATT_PAIRS, 1, 2 * tq), F32),
                        pltpu.VMEM((ATT_PAIRS, LANES, 2 * tq), F32)]
        + [pltpu.VMEM((ATT_PAIRS, KEY_STEP * tq, 2 * tq), F32)] * 2
        + [pltpu.VMEM((ATT_PAIRS, 1, 2 * tq), F32)] * 2,
        compiler_params=_params(("parallel", "parallel", "arbitrary")),
        name="moba_attention",
    )(q_aug, k_aug, v_t)
    return out.reshape(batch * seq, ATT_WIDTH)


def _merge_body(x_ref, h_ref, ya_ref, yb_ref, yc_ref, wgate_ref, gb_ref, wbr_ref, wout_ref, out_ref):
    h = h_ref[...]
    d = x_ref.shape[1]
    merged = None
    for n, y_ref in enumerate((ya_ref, yb_ref, yc_ref)):
        gate = _sigmoid(_dot(h, wgate_ref[:, n * d:(n + 1) * d]) + gb_ref[:, n * d:(n + 1) * d])
        term = gate * _dot(y_ref[...], wbr_ref[n])
        merged = term if merged is None else merged + term
    out_ref[...] = x_ref[...] + _dot(merged.astype(BF16), wout_ref[...])


def _merge(x, h, ya, yb, yc, wgate, gb, wbr, wout, *, tm=512):
    t, d = x.shape
    c = CONV_WIDTH
    row_d = pl.BlockSpec((tm, d), lambda i: (i, 0))
    row_c = pl.BlockSpec((tm, c), lambda i: (i, 0))
    return pl.pallas_call(
        _merge_body,
        grid=(t // tm,),
        in_specs=[
            row_d, row_d, row_c, row_c, row_c,
            pl.BlockSpec((d, N_BRANCH * d), lambda i: (0, 0)),
            pl.BlockSpec((1, N_BRANCH * d), lambda i: (0, 0)),
            pl.BlockSpec((N_BRANCH, c, d), lambda i: (0, 0, 0)),
            pl.BlockSpec((d, d), lambda i: (0, 0)),
        ],
        out_specs=row_d,
        out_shape=jax.ShapeDtypeStruct((t, d), F32),
        compiler_params=_params(("parallel",)),
        name="merge",
    )(x, h, ya, yb, yc, wgate, gb, wbr, wout)


def _rope_tables(seq):
    pos = jnp.arange(seq, dtype=F32)
    inv_freq = ROPE_THETA ** (-jnp.arange(0, ROPE_DIM, 2, dtype=F32) / ROPE_DIM)
    ang = pos[:, None] * inv_freq[None, :]
    cos, sin = jnp.cos(ang), jnp.sin(ang)
    half = ROPE_DIM // 2
    ones = jnp.ones((seq, HEAD_DIM - ROPE_DIM), F32)
    zeros = jnp.zeros((seq, HEAD_DIM - ROPE_DIM), F32)
    zh = jnp.zeros((seq, half), F32)
    rc = jnp.concatenate([cos, cos, ones], axis=1)
    rs1 = jnp.concatenate([-sin, zh, zeros], axis=1)
    rs2 = jnp.concatenate([zh, sin, zeros], axis=1)
    reps = LANES // HEAD_DIM
    return tuple(jnp.tile(a, (1, reps)) for a in (rc, rs1, rs2)) + (cos.T, sin.T)


def kernel(x, ffn1_norm, ffn1_wi, ffn1_wo, mix_norm, w_in, conv_w, conv_b, conv_ln_g, conv_ln_b,
           sgu_ln_g, sgu_ln_b, sgu_w, sgu_b, w_branch, gate_b, w_out, ffn2_norm, ffn2_wi, ffn2_wo,
           final_norm):
    batch, seq, d = x.shape
    depth = ffn1_norm.shape[0]
    assert seq % MOBA_BLOCK == 0 and seq % 512 == 0 and d % LANES == 0
    assert KEY_STEP <= seq // MOBA_BLOCK <= MAX_BLOCKS
    xt = x.reshape(batch * seq, d)
    rc, rs1, rs2, cos_t, sin_t = _rope_tables(seq)
    gw = SGU_WIDTH // SGU_GROUPS

    o_sgu = 2 * CONV_WIDTH
    o_q = o_sgu + 2 * SGU_WIDTH
    o_k, o_v, o_gate = o_q + ATT_WIDTH, o_q + 2 * ATT_WIDTH, o_q + 3 * ATT_WIDTH

    def row(a):
        return a.reshape(1, -1)

    for l in range(depth):
        wl = w_in[l]

        def cols(a, b):
            return wl[:, a:b].astype(BF16)

        xt, h = _ffn(xt, row(ffn1_norm[l]), ffn1_wi[l].astype(BF16), ffn1_wo[l].astype(BF16),
                     row(mix_norm[l]), post="both")

        cw8 = jnp.broadcast_to(conv_w[l][:, None, :], (CONV_K, SUBLANES, CONV_WIDTH))
        ya = _conv_branch(h, cols(0, CONV_WIDTH), cols(CONV_WIDTH, o_sgu), cw8, row(conv_b[l]),
                          row(conv_ln_g[l]), row(conv_ln_b[l]), seq=seq)

        bs_full = jnp.repeat(sgu_b[l].T, gw, axis=1)
        yb = _sgu_branch(h, cols(o_sgu, o_sgu + SGU_WIDTH), cols(o_sgu + SGU_WIDTH, o_q),
                         row(sgu_ln_g[l]), row(sgu_ln_b[l]), sgu_w[l], bs_full)

        wqv_t = jnp.concatenate([wl[:, o_q:o_k], wl[:, o_v:o_gate]], axis=1).T.astype(BF16)
        q_aug_t, k_aug, v_t = _qkv_route(h, wqv_t, cols(o_k, o_v), rc, rs1, rs2, cos_t, sin_t,
                                         batch=batch, seq=seq)
        yc = _moba_attention(q_aug_t, k_aug, v_t, batch=batch, seq=seq)

        xt = _merge(xt, h, ya, yb, yc, cols(o_gate, wl.shape[1]), row(gate_b[l]),
                    w_branch[l].astype(BF16), w_out[l].astype(BF16))

        last = l == depth - 1
        xt = _ffn(xt, row(ffn2_norm[l]), ffn2_wi[l].astype(BF16), ffn2_wo[l].astype(BF16),
                  row(final_norm), post="norm_only" if last else "none")
    return xt.reshape(batch, seq, d)
```

```python
import functools

import numpy as np
import jax
import jax.numpy as jnp
from jax import lax
from jax.experimental import pallas as pl
from jax.experimental.pallas import tpu as pltpu

D_FF = 2816
CONV_WIDTH = 512
CONV_K = 31
SGU_WIDTH = 512
SGU_GROUPS = 4
SGU_CHUNK = 128
ATT_HEADS = 8
HEAD_DIM = 64
ATT_WIDTH = ATT_HEADS * HEAD_DIM
MOBA_BLOCK = 256
MOBA_TOPK = 3
ROPE_THETA = 500000.0
ROPE_DIM = HEAD_DIM // 4
N_BRANCH = 3
EPS = 1e-6

LANES = 128
SUBLANES = 8
VMEM_LIMIT = 56 * 1024 * 1024
MASK_VALUE = -1e30
CONV_HALO = 32
CONV_ROWS = 32
QKV_BLOCKS = 4
ATT_PAIRS = 4
KEY_STEP = 4
MAX_BLOCKS = 32
AUG = 2 * LANES

BF16 = jnp.bfloat16
F32 = jnp.float32
NT_DIMS = (((1,), (1,)), ((), ()))


def _dot(a, b):
    return jnp.dot(a, b, preferred_element_type=F32)


def _sigmoid(x):
    return 1.0 / (1.0 + jnp.exp(-x))


def _rms(x, g):
    return x * lax.rsqrt(jnp.mean(x * x, axis=-1, keepdims=True) + EPS) * g


def _params(sem, flags=None):
    return pltpu.CompilerParams(dimension_semantics=sem, vmem_limit_bytes=VMEM_LIMIT, flags=flags)


def _ffn_body(x_ref, g_ref, wg_ref, wu_ref, wo_ref, g2_ref, *refs, post, single_step):
    if post == "both":
        out_ref, nxt_ref = refs[:2]
    else:
        out_ref = refs[0]

    def finish(acc, rows=slice(None)):
        y = x_ref[rows, :] + 0.5 * acc
        if post == "both":
            out_ref[rows, :] = y
            nxt_ref[rows, :] = _rms(y, g2_ref[...]).astype(nxt_ref.dtype)
        elif post == "norm_only":
            out_ref[rows, :] = _rms(y, g2_ref[...])
        else:
            out_ref[rows, :] = y

    def partial_out(xn):
        gate = _dot(xn, wg_ref[...])
        up = _dot(xn, wu_ref[...])
        act = (gate * _sigmoid(gate) * up).astype(BF16)
        return _dot(act, wo_ref[...])

    if single_step:
        half = x_ref.shape[0] // 2
        for c in range(2):
            rows = slice(c * half, (c + 1) * half)
            finish(partial_out(_rms(x_ref[rows, :], g_ref[...]).astype(BF16)), rows)
        return

    xn_ref, acc_ref = refs[-2:]
    j = pl.program_id(1)

    @pl.when(j == 0)
    def _():
        xn_ref[...] = _rms(x_ref[...], g_ref[...]).astype(BF16)

    part = partial_out(xn_ref[...])

    @pl.when(j == 0)
    def _():
        acc_ref[...] = part

    @pl.when(j > 0)
    def _():
        acc_ref[...] += part

    @pl.when(j == pl.num_programs(1) - 1)
    def _():
        finish(acc_ref[...])


def _ffn(x, g, wi, wo, g2, *, post, tm=512, tf=D_FF):
    t, d = x.shape
    nf = D_FF // tf
    single = nf == 1
    w_mode = dict(pipeline_mode=pl.Buffered(1)) if single else {}
    row = pl.BlockSpec((tm, d), lambda i, j: (i, 0))
    vec = pl.BlockSpec((1, d), lambda i, j: (0, 0))
    in_specs = [
        row, vec,
        pl.BlockSpec((d, tf), lambda i, j: (0, j), **w_mode),
        pl.BlockSpec((d, tf), lambda i, j: (0, j + nf), **w_mode),
        pl.BlockSpec((tf, d), lambda i, j: (j, 0), **w_mode),
        vec,
    ]
    if post == "both":
        out_shape = (jax.ShapeDtypeStruct((t, d), F32), jax.ShapeDtypeStruct((t, d), BF16))
        out_specs = (row, row)
    else:
        out_shape = jax.ShapeDtypeStruct((t, d), F32)
        out_specs = row
    return pl.pallas_call(
        functools.partial(_ffn_body, post=post, single_step=single),
        grid=(t // tm, nf),
        in_specs=in_specs,
        out_specs=out_specs,
        out_shape=out_shape,
        scratch_shapes=[] if single else [pltpu.VMEM((tm, d), BF16), pltpu.VMEM((tm, d), F32)],
        compiler_params=_params(("parallel", "arbitrary")),
        name="ffn",
    )(x, g, wi, wi, wo, g2)


def _conv_body(h_ref, wa_ref, wg_ref, cw_ref, cb_ref, lg_ref, lb_ref, out_ref, buf_ref, conv_ref, *, tm,
               tiles_per_seq):
    t = pl.program_id(0)

    @pl.when(t % tiles_per_seq == 0)
    def _():
        buf_ref[...] = jnp.zeros(buf_ref.shape, F32)

    n_chunks = 4
    rows = tm // n_chunks
    for c in range(n_chunks):
        h = h_ref[c * rows:(c + 1) * rows, :]
        glu = _dot(h, wa_ref[...]) * _sigmoid(_dot(h, wg_ref[...]))
        for r in range(SUBLANES):
            buf_ref[r, CONV_HALO - r + c * rows:CONV_HALO - r + (c + 1) * rows, :] = glu

    first_tap = CONV_HALO - (CONV_K - 1)

    def step(i, carry):
        base = pl.multiple_of(i * CONV_ROWS, CONV_ROWS)
        y = None
        for k in range(CONV_K):
            o = first_tap + k
            win = buf_ref[o % SUBLANES, pl.ds(base + (o - o % SUBLANES), CONV_ROWS), :]
            term = win.reshape(CONV_ROWS // SUBLANES, SUBLANES, CONV_WIDTH) * cw_ref[k]
            y = term if y is None else y + term
        conv_ref[pl.ds(base, CONV_ROWS), :] = y.reshape(CONV_ROWS, CONV_WIDTH)
        return carry

    lax.fori_loop(0, tm // CONV_ROWS, step, 0)
    for r in range(SUBLANES):
        buf_ref[r, 0:CONV_HALO, :] = buf_ref[r, tm:tm + CONV_HALO, :]

    y = conv_ref[...] + cb_ref[...]
    mu = jnp.mean(y, axis=-1, keepdims=True)
    yc = y - mu
    var = jnp.mean(yc * yc, axis=-1, keepdims=True)
    yn = yc * lax.rsqrt(var + EPS) * lg_ref[...] + lb_ref[...]
    out_ref[...] = (yn * _sigmoid(yn)).astype(out_ref.dtype)


def _conv_branch(h, wa, wg, cw, cb, lg, lb, *, seq, tm=512):
    t, d = h.shape
    c = CONV_WIDTH
    vec = pl.BlockSpec((1, c), lambda i: (0, 0))
    return pl.pallas_call(
        functools.partial(_conv_body, tm=tm, tiles_per_seq=seq // tm),
        grid=(t // tm,),
        in_specs=[
            pl.BlockSpec((tm, d), lambda i: (i, 0)),
            pl.BlockSpec((d, c), lambda i: (0, 0)),
            pl.BlockSpec((d, c), lambda i: (0, 0)),
            pl.BlockSpec((CONV_K, SUBLANES, c), lambda i: (0, 0, 0)),
            vec, vec, vec,
        ],
        out_specs=pl.BlockSpec((tm, c), lambda i: (i, 0)),
        out_shape=jax.ShapeDtypeStruct((t, c), BF16),
        scratch_shapes=[pltpu.VMEM((SUBLANES, CONV_HALO + tm, c), F32), pltpu.VMEM((tm, c), F32)],
        compiler_params=_params(("arbitrary",)),
        name="conv_branch",
    )(h, wa, wg, cw, cb, lg, lb)


def _gelu_tanh(x):
    c = np.float32(np.sqrt(2.0 / np.pi))
    return x * (0.5 * (1.0 + jnp.tanh(c * (x + 0.044715 * (x * x * x)))))


def _sgu_body(h_ref, wu_ref, wv_ref, lg_ref, lb_ref, ws_ref, bs_ref, out_ref, *, tm):
    h = h_ref[...]
    u = _gelu_tanh(_dot(h, wu_ref[...]))
    v = _gelu_tanh(_dot(h, wv_ref[...]))
    mu = jnp.mean(v, axis=-1, keepdims=True)
    vc = v - mu
    var = jnp.mean(vc * vc, axis=-1, keepdims=True)
    vn = (vc * lax.rsqrt(var + EPS) * lg_ref[...] + lb_ref[...]).astype(BF16)

    n_chunks = tm // SGU_CHUNK
    gw = SGU_WIDTH // SGU_GROUPS
    row = lax.broadcasted_iota(jnp.int32, (SGU_CHUNK, SGU_CHUNK), 0)
    col = lax.broadcasted_iota(jnp.int32, (SGU_CHUNK, SGU_CHUNK), 1)
    bias = bs_ref[...]
    mixed = [[None] * SGU_GROUPS for _ in range(n_chunks)]
    for g in range(SGU_GROUPS):
        w = jnp.where(col <= row, ws_ref[g], 0.0).astype(BF16)
        rhs = jnp.concatenate(
            [vn[n * SGU_CHUNK:(n + 1) * SGU_CHUNK, g * gw:(g + 1) * gw] for n in range(n_chunks)], axis=1)
        res = _dot(w, rhs)
        for n in range(n_chunks):
            mixed[n][g] = res[:, n * gw:(n + 1) * gw]
    for n in range(n_chunks):
        m = jnp.concatenate(mixed[n], axis=1) + bias
        rows = slice(n * SGU_CHUNK, (n + 1) * SGU_CHUNK)
        out_ref[rows, :] = (u[rows, :] * m).astype(out_ref.dtype)


def _sgu_branch(h, wu, wv, lg, lb, ws, bs_full, *, tm=512):
    t, d = h.shape
    c = SGU_WIDTH
    vec = pl.BlockSpec((1, c), lambda i: (0, 0))
    return pl.pallas_call(
        functools.partial(_sgu_body, tm=tm),
        grid=(t // tm,),
        in_specs=[
            pl.BlockSpec((tm, d), lambda i: (i, 0)),
            pl.BlockSpec((d, c), lambda i: (0, 0)),
            pl.BlockSpec((d, c), lambda i: (0, 0)),
            vec, vec,
            pl.BlockSpec((SGU_GROUPS, SGU_CHUNK, SGU_CHUNK), lambda i: (0, 0, 0)),
            pl.BlockSpec((SGU_CHUNK, c), lambda i: (0, 0)),
        ],
        out_specs=pl.BlockSpec((tm, c), lambda i: (i, 0)),
        out_shape=jax.ShapeDtypeStruct((t, c), BF16),
        compiler_params=_params(("parallel",)),
        name="sgu_branch",
    )(h, wu, wv, lg, lb, ws, bs_full)


def _qkv_body(h_ref, wqv_ref, wk_ref, rc_ref, rs1_ref, rs2_ref, ct_ref, st_ref, q_ref, k_ref, v_ref, km_ref):
    step = pl.program_id(1)

    @pl.when(step == 0)
    def _():
        km_ref[...] = jnp.zeros(km_ref.shape, F32)

    h = h_ref[...]
    qv_all = lax.dot_general(wqv_ref[...], h, NT_DIMS, preferred_element_type=F32)
    k_all = _dot(h, wk_ref[...])

    half = ROPE_DIM // 2
    tokens = MOBA_BLOCK
    scale = HEAD_DIM ** -0.5
    neg_inf = float("-inf")

    lane = lax.broadcasted_iota(jnp.int32, (tokens, LANES), 1)
    km_lane = lax.broadcasted_iota(jnp.int32, (MAX_BLOCKS, LANES), 1)
    blk_row = lax.broadcasted_iota(jnp.int32, (MAX_BLOCKS, tokens), 0)
    blk_row_f = blk_row.astype(F32)
    zeros_head = jnp.zeros((HEAD_DIM, tokens), F32)
    zeros_pad = jnp.zeros((AUG - 2 * HEAD_DIM - MAX_BLOCKS, 2 * tokens), F32)

    for sub in range(QKV_BLOCKS):
        blk = step * QKV_BLOCKS + sub
        rows = slice(sub * tokens, (sub + 1) * tokens)
        qv = qv_all[:, rows]
        k = k_all[rows, :]
        v_ref[0, sub] = qv[ATT_WIDTH:, :].astype(v_ref.dtype)
        rc, rs1, rs2 = rc_ref[rows, :], rs1_ref[rows, :], rs2_ref[rows, :]
        cos_t, sin_t = ct_ref[:, rows], st_ref[:, rows]
        onehot = jnp.where(lane == blk, 1.0, 0.0)

        def rope_t(x):
            x1, x2 = x[:half], x[half:ROPE_DIM]
            return jnp.concatenate([x1 * cos_t - x2 * sin_t, x2 * cos_t + x1 * sin_t, x[ROPE_DIM:]], axis=0)

        def route(gate):
            g = jnp.where(blk_row < blk, gate, neg_inf)
            bias = jnp.where(blk_row == blk, 0.0, MASK_VALUE)
            for _ in range(MOBA_TOPK):
                m = jnp.max(g, axis=0, keepdims=True)
                idx = jnp.min(jnp.where(g == m, blk_row_f, 2.0 * MAX_BLOCKS), axis=0, keepdims=True)
                pick = jnp.logical_and(blk_row_f == idx, m > neg_inf)
                bias = jnp.where(pick, 0.0, bias)
                g = jnp.where(pick, neg_inf, g)
            return bias

        for pair in range(ATT_HEADS // 2):
            even, odd = 2 * pair, 2 * pair + 1
            x = k[:, pair * LANES:(pair + 1) * LANES]
            ks = x * rc + pltpu.roll(x, LANES - half, 1) * rs1 + pltpu.roll(x, half, 1) * rs2
            k_ref[0, pair, rows, :] = jnp.concatenate([ks, onehot], axis=1).astype(k_ref.dtype)

            q_even = rope_t(qv[even * HEAD_DIM:(even + 1) * HEAD_DIM, :]) * scale
            q_odd = rope_t(qv[odd * HEAD_DIM:(odd + 1) * HEAD_DIM, :]) * scale
            q_pair = jnp.concatenate([q_even, q_odd], axis=0)

            km = km_ref[pair]
            gate_even = jnp.dot(jnp.where(km_lane < HEAD_DIM, km, 0.0), q_pair,
                                precision=lax.Precision.HIGHEST, preferred_element_type=F32)
            gate_odd = jnp.dot(jnp.where(km_lane >= HEAD_DIM, km, 0.0), q_pair,
                               precision=lax.Precision.HIGHEST, preferred_element_type=F32)
            q_ref[0, pair, sub] = jnp.concatenate([
                jnp.concatenate([q_even, zeros_head], axis=1),
                jnp.concatenate([zeros_head, q_odd], axis=1),
                jnp.concatenate([route(gate_even), route(gate_odd)], axis=1),
                zeros_pad], axis=0).astype(q_ref.dtype)

            km_ref[pair, pl.ds(blk, 1), :] = jnp.sum(ks, axis=0, keepdims=True) * (1.0 / MOBA_BLOCK)


def _qkv_route(h, wqv_t, wk, rc, rs1, rs2, cos_t, sin_t, *, batch, seq):
    t, d = h.shape
    blk_tokens = MOBA_BLOCK
    tm = QKV_BLOCKS * blk_tokens
    nb = seq // blk_tokens
    ns = seq // tm
    pairs = ATT_HEADS // 2
    r_spec = pl.BlockSpec((tm, LANES), lambda b, i: (i, 0))
    t_spec = pl.BlockSpec((ROPE_DIM // 2, tm), lambda b, i: (0, i))
    return pl.pallas_call(
        _qkv_body,
        grid=(batch, ns),
        in_specs=[pl.BlockSpec((tm, d), lambda b, i: (b * ns + i, 0)),
                  pl.BlockSpec((2 * ATT_WIDTH, d), lambda b, i: (0, 0)),
                  pl.BlockSpec((d, ATT_WIDTH), lambda b, i: (0, 0)),
                  r_spec, r_spec, r_spec, t_spec, t_spec],
        out_specs=(pl.BlockSpec((1, pairs, QKV_BLOCKS, AUG, 2 * blk_tokens), lambda b, i: (b, 0, i, 0, 0)),
                   pl.BlockSpec((1, pairs, tm, AUG), lambda b, i: (b, 0, i, 0)),
                   pl.BlockSpec((1, QKV_BLOCKS, ATT_WIDTH, blk_tokens), lambda b, i: (b, i, 0, 0))),
        out_shape=(jax.ShapeDtypeStruct((batch, pairs, nb, AUG, 2 * blk_tokens), BF16),
                   jax.ShapeDtypeStruct((batch, pairs, seq, AUG), BF16),
                   jax.ShapeDtypeStruct((batch, nb, ATT_WIDTH, blk_tokens), BF16)),
        scratch_shapes=[pltpu.VMEM((ATT_HEADS // 2, MAX_BLOCKS, LANES), F32)],
        compiler_params=_params(("arbitrary", "arbitrary")),
        name="qkv_route",
    )(h, wqv_t, wk, rc, rs1, rs2, cos_t, sin_t)


def _attn_body(q_ref, k_ref, v_ref, o_ref, m_ref, l_ref, acc_ref, sa_ref, sb_ref, ma_ref, mb_ref):
    blk = pl.program_id(2)
    tq = MOBA_BLOCK
    key = lax.broadcasted_iota(jnp.int32, (tq, 2 * tq), 0)
    lane = lax.broadcasted_iota(jnp.int32, (tq, 2 * tq), 1)
    qry = jnp.where(lane < tq, lane, lane - tq)
    causal = jnp.where(key <= qry, 0.0, MASK_VALUE)

    def scores(pp, first_blk, n_blk):
        start = pl.multiple_of(first_blk * tq, tq)
        return _dot(k_ref[0, pp, pl.ds(start, n_blk * tq), :], q_ref[0, pp, 0])

    def update(pp, s, first_blk, n_blk, m_cur=None):
        v_t = jnp.concatenate([v_ref[0, first_blk + a, pp * LANES:(pp + 1) * LANES, :]
                               for a in range(n_blk)], axis=1)
        if m_cur is None:
            m_cur = jnp.max(s, axis=0, keepdims=True)
        m_old = m_ref[pp]
        m_new = jnp.maximum(m_old, m_cur)
        alpha = jnp.exp(m_old - m_new)
        p = jnp.exp(s - m_new)
        l_ref[pp] = alpha * l_ref[pp] + jnp.sum(p, axis=0, keepdims=True)
        m_ref[pp] = m_new
        acc_ref[pp] = alpha * acc_ref[pp] + _dot(v_t, p.astype(BF16))

    for pp in range(ATT_PAIRS):
        m_ref[pp] = jnp.full((1, 2 * tq), MASK_VALUE, F32)
        l_ref[pp] = jnp.zeros((1, 2 * tq), F32)
        acc_ref[pp] = jnp.zeros((LANES, 2 * tq), F32)

    n_full = blk // KEY_STEP
    n_rem = blk - n_full * KEY_STEP

    def fill(bufs, grp, with_max=True):
        buf, mbuf = bufs
        ss = [scores(pp, grp * KEY_STEP, KEY_STEP) for pp in range(ATT_PAIRS)]
        for pp in range(ATT_PAIRS):
            buf[pp] = ss[pp]
            if with_max:
                mbuf[pp] = jnp.max(ss[pp], axis=0, keepdims=True)

    def drain(bufs, grp):
        buf, mbuf = bufs
        for pp in range(ATT_PAIRS):
            update(pp, buf[pp], grp * KEY_STEP, KEY_STEP, mbuf[pp])

    def drain_head(bufs, rem):
        buf, _ = bufs
        mask = causal if rem == 0 else jnp.concatenate(
            [jnp.zeros((rem * tq, 2 * tq), F32), causal], axis=0)
        for pp in range(ATT_PAIRS):
            update(pp, buf[pp, 0:(rem + 1) * tq, :] + mask, blk - rem, rem + 1)

    sa_ref, sb_ref = (sa_ref, ma_ref), (sb_ref, mb_ref)

    @pl.when(n_full > 0)
    def _():
        fill(sa_ref, 0)

    def two_groups(t, carry):
        grp = 2 * t
        fill(sb_ref, grp + 1)
        drain(sa_ref, grp)
        fill(sa_ref, grp + 2)
        drain(sb_ref, grp + 1)
        return carry

    n_trips = (n_full - 1) // 2
    lax.fori_loop(0, n_trips, two_groups, 0)
    grp_left = 2 * n_trips
    n_left = n_full - grp_left
    one_left = jnp.logical_and(n_full > 0, n_left == 1)

    @pl.when(one_left)
    def _():
        fill(sb_ref, n_full, with_max=False)
        drain(sa_ref, grp_left)

    @pl.when(jnp.logical_and(n_full > 0, n_left == 2))
    def _():
        fill(sb_ref, grp_left + 1)
        drain(sa_ref, grp_left)
        fill(sa_ref, n_full, with_max=False)
        drain(sb_ref, grp_left + 1)

    @pl.when(n_full == 0)
    def _():
        fill(sa_ref, 0, with_max=False)

    for rem in range(KEY_STEP):
        for bufs, in_b in ((sa_ref, False), (sb_ref, True)):
            @pl.when(jnp.logical_and(n_rem == rem, one_left if in_b else jnp.logical_not(one_left)))
            def _():
                drain_head(bufs, rem)

    for pp in range(ATT_PAIRS):
        inv_l = 1.0 / l_ref[pp]
        o = acc_ref[pp] * inv_l
        o_t = jnp.concatenate([o[:HEAD_DIM, :tq], o[HEAD_DIM:, tq:]], axis=0)
        o_ref[0, :, pp * LANES:(pp + 1) * LANES] = o_t.T.astype(o_ref.dtype)


def _moba_attention(q_aug, k_aug, v_t, *, batch, seq):
    tq = MOBA_BLOCK
    nb = seq // tq
    gw = ATT_PAIRS * LANES
    out = pl.pallas_call(
        _attn_body,
        grid=(batch, ATT_HEADS // (2 * ATT_PAIRS), nb),
        in_specs=[
            pl.BlockSpec((1, ATT_PAIRS, 1, AUG, 2 * tq), lambda b, g, i: (b, g, i, 0, 0)),
            pl.BlockSpec((1, ATT_PAIRS, seq, AUG), lambda b, g, i: (b, g, 0, 0), pipeline_mode=pl.Buffered(1)),
            pl.BlockSpec((1, nb, gw, tq), lambda b, g, i: (b, 0, g, 0), pipeline_mode=pl.Buffered(1)),
        ],
        out_specs=pl.BlockSpec((1, tq, gw), lambda b, g, i: (b, i, g)),
        out_shape=jax.ShapeDtypeStruct((batch, seq, ATT_WIDTH), BF16),
        scratch_shapes=[pltpu.VMEM((ATT_PAIRS, 1, 2 * tq), F32), pltpu.VMEM((ATT_PAIRS, 1, 2 * tq), F32),
                        pltpu.VMEM((ATT_PAIRS, LANES, 2 * tq), F32)]
        + [pltpu.VMEM((ATT_PAIRS, KEY_STEP * tq, 2 * tq), F32)] * 2
        + [pltpu.VMEM((ATT_PAIRS, 1, 2 * tq), F32)] * 2,
        compiler_params=_params(("parallel", "parallel", "arbitrary")),
        name="moba_attention",
    )(q_aug, k_aug, v_t)
    return out.reshape(batch * seq, ATT_WIDTH)


def _merge_body(x_ref, h_ref, ya_ref, yb_ref, yc_ref, wgate_ref, gb_ref, wbr_ref, wout_ref, out_ref):
    h = h_ref[...]
    d = x_ref.shape[1]
    merged = None
    for n, y_ref in enumerate((ya_ref, yb_ref, yc_ref)):
        gate = _sigmoid(_dot(h, wgate_ref[:, n * d:(n + 1) * d]) + gb_ref[:, n * d:(n + 1) * d])
        term = gate * _dot(y_ref[...], wbr_ref[n])
        merged = term if merged is None else merged + term
    out_ref[...] = x_ref[...] + _dot(merged.astype(BF16), wout_ref[...])


def _merge(x, h, ya, yb, yc, wgate, gb, wbr, wout, *, tm=512):
    t, d = x.shape
    c = CONV_WIDTH
    row_d = pl.BlockSpec((tm, d), lambda i: (i, 0))
    row_c = pl.BlockSpec((tm, c), lambda i: (i, 0))
    return pl.pallas_call(
        _merge_body,
        grid=(t // tm,),
        in_specs=[
            row_d, row_d, row_c, row_c, row_c,
            pl.BlockSpec((d, N_BRANCH * d), lambda i: (0, 0)),
            pl.BlockSpec((1, N_BRANCH * d), lambda i: (0, 0)),
            pl.BlockSpec((N_BRANCH, c, d), lambda i: (0, 0, 0)),
            pl.BlockSpec((d, d), lambda i: (0, 0)),
        ],
        out_specs=row_d,
        out_shape=jax.ShapeDtypeStruct((t, d), F32),
        compiler_params=_params(("parallel",)),
        name="merge",
    )(x, h, ya, yb, yc, wgate, gb, wbr, wout)


def _rope_tables(seq):
    pos = jnp.arange(seq, dtype=F32)
    inv_freq = ROPE_THETA ** (-jnp.arange(0, ROPE_DIM, 2, dtype=F32) / ROPE_DIM)
    ang = pos[:, None] * inv_freq[None, :]
    cos, sin = jnp.cos(ang), jnp.sin(ang)
    half = ROPE_DIM // 2
    ones = jnp.ones((seq, HEAD_DIM - ROPE_DIM), F32)
    zeros = jnp.zeros((seq, HEAD_DIM - ROPE_DIM), F32)
    zh = jnp.zeros((seq, half), F32)
    rc = jnp.concatenate([cos, cos, ones], axis=1)
    rs1 = jnp.concatenate([-sin, zh, zeros], axis=1)
    rs2 = jnp.concatenate([zh, sin, zeros], axis=1)
    reps = LANES // HEAD_DIM
    return tuple(jnp.tile(a, (1, reps)) for a in (rc, rs1, rs2)) + (cos.T, sin.T)


def kernel(x, ffn1_norm, ffn1_wi, ffn1_wo, mix_norm, w_in, conv_w, conv_b, conv_ln_g, conv_ln_b,
           sgu_ln_g, sgu_ln_b, sgu_w, sgu_b, w_branch, gate_b, w_out, ffn2_norm, ffn2_wi, ffn2_wo,
           final_norm):
    batch, seq, d = x.shape
    depth = ffn1_norm.shape[0]
    assert seq % MOBA_BLOCK == 0 and seq % 512 == 0 and d % LANES == 0
    assert seq // MOBA_BLOCK <= MAX_BLOCKS and (seq // MOBA_BLOCK) % KEY_STEP == 0
    xt = x.reshape(batch * seq, d)
    rc, rs1, rs2, cos_t, sin_t = _rope_tables(seq)
    gw = SGU_WIDTH // SGU_GROUPS

    o_sgu = 2 * CONV_WIDTH
    o_q = o_sgu + 2 * SGU_WIDTH
    o_k, o_v, o_gate = o_q + ATT_WIDTH, o_q + 2 * ATT_WIDTH, o_q + 3 * ATT_WIDTH

    def row(a):
        return a.reshape(1, -1)

    for l in range(depth):
        wl = w_in[l]

        def cols(a, b):
            return wl[:, a:b].astype(BF16)

        xt, h = _ffn(xt, row(ffn1_norm[l]), ffn1_wi[l].astype(BF16), ffn1_wo[l].astype(BF16),
                     row(mix_norm[l]), post="both")

        cw8 = jnp.broadcast_to(conv_w[l][:, None, :], (CONV_K, SUBLANES, CONV_WIDTH))
        ya = _conv_branch(h, cols(0, CONV_WIDTH), cols(CONV_WIDTH, o_sgu), cw8, row(conv_b[l]),
                          row(conv_ln_g[l]), row(conv_ln_b[l]), seq=seq)

        bs_full = jnp.repeat(sgu_b[l].T, gw, axis=1)
        yb = _sgu_branch(h, cols(o_sgu, o_sgu + SGU_WIDTH), cols(o_sgu + SGU_WIDTH, o_q),
                         row(sgu_ln_g[l]), row(sgu_ln_b[l]), sgu_w[l], bs_full)

        wqv_t = jnp.concatenate([wl[:, o_q:o_k], wl[:, o_v:o_gate]], axis=1).T.astype(BF16)
        q_aug_t, k_aug, v_t = _qkv_route(h, wqv_t, cols(o_k, o_v), rc, rs1, rs2, cos_t, sin_t,
                                         batch=batch, seq=seq)
        yc = _moba_attention(q_aug_t, k_aug, v_t, batch=batch, seq=seq)

        xt = _merge(xt, h, ya, yb, yc, cols(o_gate, wl.shape[1]), row(gate_b[l]),
                    w_branch[l].astype(BF16), w_out[l].astype(BF16))

        last = l == depth - 1
        xt = _ffn(xt, row(ffn2_norm[l]), ffn2_wi[l].astype(BF16), ffn2_wo[l].astype(BF16),
                  row(final_norm), post="norm_only" if last else "none")
    return xt.reshape(batch, seq, d)
```

```python
import functools

import numpy as np
import jax
import jax.numpy as jnp
from jax import lax
from jax.experimental import pallas as pl
from jax.experimental.pallas import tpu as pltpu

D_FF = 2816
CONV_WIDTH = 512
CONV_K = 31
SGU_WIDTH = 512
SGU_GROUPS = 4
SGU_CHUNK = 128
ATT_HEADS = 8
HEAD_DIM = 64
ATT_WIDTH = ATT_HEADS * HEAD_DIM
MOBA_BLOCK = 256
MOBA_TOPK = 3
ROPE_THETA = 500000.0
ROPE_DIM = HEAD_DIM // 4
N_BRANCH = 3
EPS = 1e-6

LANES = 128
SUBLANES = 8
VMEM_LIMIT = 56 * 1024 * 1024
MASK_VALUE = -1e30
CONV_HALO = 32
CONV_ROWS = 64
QKV_BLOCKS = 4
ATT_PAIRS = 4
KEY_STEP = 4
MAX_BLOCKS = 32
AUG = 2 * LANES

BF16 = jnp.bfloat16
F32 = jnp.float32
NT_DIMS = (((1,), (1,)), ((), ()))


def _dot(a, b):
    return jnp.dot(a, b, preferred_element_type=F32)


def _sigmoid(x):
    return 1.0 / (1.0 + jnp.exp(-x))


def _rms(x, g):
    return x * lax.rsqrt(jnp.mean(x * x, axis=-1, keepdims=True) + EPS) * g


def _params(sem, flags=None):
    return pltpu.CompilerParams(dimension_semantics=sem, vmem_limit_bytes=VMEM_LIMIT, flags=flags)


def _ffn_body(x_ref, g_ref, wg_ref, wu_ref, wo_ref, g2_ref, *refs, post, single_step):
    if post == "both":
        out_ref, nxt_ref = refs[:2]
    else:
        out_ref = refs[0]

    def finish(acc, rows=slice(None)):
        y = x_ref[rows, :] + 0.5 * acc
        if post == "both":
            out_ref[rows, :] = y
            nxt_ref[rows, :] = _rms(y, g2_ref[...]).astype(nxt_ref.dtype)
        elif post == "norm_only":
            out_ref[rows, :] = _rms(y, g2_ref[...])
        else:
            out_ref[rows, :] = y

    def partial_out(xn):
        gate = _dot(xn, wg_ref[...])
        up = _dot(xn, wu_ref[...])
        act = (gate * _sigmoid(gate) * up).astype(BF16)
        return _dot(act, wo_ref[...])

    if single_step:
        half = x_ref.shape[0] // 2
        for c in range(2):
            rows = slice(c * half, (c + 1) * half)
            finish(partial_out(_rms(x_ref[rows, :], g_ref[...]).astype(BF16)), rows)
        return

    xn_ref, acc_ref = refs[-2:]
    j = pl.program_id(1)

    @pl.when(j == 0)
    def _():
        xn_ref[...] = _rms(x_ref[...], g_ref[...]).astype(BF16)

    part = partial_out(xn_ref[...])

    @pl.when(j == 0)
    def _():
        acc_ref[...] = part

    @pl.when(j > 0)
    def _():
        acc_ref[...] += part

    @pl.when(j == pl.num_programs(1) - 1)
    def _():
        finish(acc_ref[...])


def _ffn(x, g, wi, wo, g2, *, post, tm=512, tf=D_FF):
    t, d = x.shape
    nf = D_FF // tf
    single = nf == 1
    w_mode = dict(pipeline_mode=pl.Buffered(1)) if single else {}
    row = pl.BlockSpec((tm, d), lambda i, j: (i, 0))
    vec = pl.BlockSpec((1, d), lambda i, j: (0, 0))
    in_specs = [
        row, vec,
        pl.BlockSpec((d, tf), lambda i, j: (0, j), **w_mode),
        pl.BlockSpec((d, tf), lambda i, j: (0, j + nf), **w_mode),
        pl.BlockSpec((tf, d), lambda i, j: (j, 0), **w_mode),
        vec,
    ]
    if post == "both":
        out_shape = (jax.ShapeDtypeStruct((t, d), F32), jax.ShapeDtypeStruct((t, d), BF16))
        out_specs = (row, row)
    else:
        out_shape = jax.ShapeDtypeStruct((t, d), F32)
        out_specs = row
    return pl.pallas_call(
        functools.partial(_ffn_body, post=post, single_step=single),
        grid=(t // tm, nf),
        in_specs=in_specs,
        out_specs=out_specs,
        out_shape=out_shape,
        scratch_shapes=[] if single else [pltpu.VMEM((tm, d), BF16), pltpu.VMEM((tm, d), F32)],
        compiler_params=_params(("parallel", "arbitrary")),
        name="ffn",
    )(x, g, wi, wi, wo, g2)


def _conv_body(h_ref, wa_ref, wg_ref, cw_ref, cb_ref, lg_ref, lb_ref, out_ref, buf_ref, conv_ref, *, tm,
               tiles_per_seq):
    t = pl.program_id(0)

    @pl.when(t % tiles_per_seq == 0)
    def _():
        buf_ref[...] = jnp.zeros(buf_ref.shape, F32)

    h = h_ref[...]
    glu = _dot(h, wa_ref[...]) * _sigmoid(_dot(h, wg_ref[...]))
    for r in range(SUBLANES):
        buf_ref[r, CONV_HALO - r:CONV_HALO - r + tm, :] = glu

    first_tap = CONV_HALO - (CONV_K - 1)

    def step(i, carry):
        base = pl.multiple_of(i * CONV_ROWS, CONV_ROWS)
        y = None
        for k in range(CONV_K):
            o = first_tap + k
            win = buf_ref[o % SUBLANES, pl.ds(base + (o - o % SUBLANES), CONV_ROWS), :]
            term = win.reshape(CONV_ROWS // SUBLANES, SUBLANES, CONV_WIDTH) * cw_ref[k]
            y = term if y is None else y + term
        conv_ref[pl.ds(base, CONV_ROWS), :] = y.reshape(CONV_ROWS, CONV_WIDTH)
        return carry

    lax.fori_loop(0, tm // CONV_ROWS, step, 0)
    for r in range(SUBLANES):
        buf_ref[r, 0:CONV_HALO, :] = buf_ref[r, tm:tm + CONV_HALO, :]

    y = conv_ref[...] + cb_ref[...]
    mu = jnp.mean(y, axis=-1, keepdims=True)
    yc = y - mu
    var = jnp.mean(yc * yc, axis=-1, keepdims=True)
    yn = yc * lax.rsqrt(var + EPS) * lg_ref[...] + lb_ref[...]
    out_ref[...] = (yn * _sigmoid(yn)).astype(out_ref.dtype)


def _conv_branch(h, wa, wg, cw, cb, lg, lb, *, seq, tm=1024):
    t, d = h.shape
    c = CONV_WIDTH
    vec = pl.BlockSpec((1, c), lambda i: (0, 0))
    return pl.pallas_call(
        functools.partial(_conv_body, tm=tm, tiles_per_seq=seq // tm),
        grid=(t // tm,),
        in_specs=[
            pl.BlockSpec((tm, d), lambda i: (i, 0)),
            pl.BlockSpec((d, c), lambda i: (0, 0)),
            pl.BlockSpec((d, c), lambda i: (0, 0)),
            pl.BlockSpec((CONV_K, SUBLANES, c), lambda i: (0, 0, 0)),
            vec, vec, vec,
        ],
        out_specs=pl.BlockSpec((tm, c), lambda i: (i, 0)),
        out_shape=jax.ShapeDtypeStruct((t, c), BF16),
        scratch_shapes=[pltpu.VMEM((SUBLANES, CONV_HALO + tm, c), F32), pltpu.VMEM((tm, c), F32)],
        compiler_params=_params(("arbitrary",)),
        name="conv_branch",
    )(h, wa, wg, cw, cb, lg, lb)


def _gelu_tanh(x):
    c = np.float32(np.sqrt(2.0 / np.pi))
    return x * (0.5 * (1.0 + jnp.tanh(c * (x + 0.044715 * (x * x * x)))))


def _sgu_body(h_ref, wu_ref, wv_ref, lg_ref, lb_ref, ws_ref, bs_ref, out_ref, *, tm):
    h = h_ref[...]
    u = _gelu_tanh(_dot(h, wu_ref[...]))
    v = _gelu_tanh(_dot(h, wv_ref[...]))
    mu = jnp.mean(v, axis=-1, keepdims=True)
    vc = v - mu
    var = jnp.mean(vc * vc, axis=-1, keepdims=True)
    vn = (vc * lax.rsqrt(var + EPS) * lg_ref[...] + lb_ref[...]).astype(BF16)

    n_chunks = tm // SGU_CHUNK
    gw = SGU_WIDTH // SGU_GROUPS
    row = lax.broadcasted_iota(jnp.int32, (SGU_CHUNK, SGU_CHUNK), 0)
    col = lax.broadcasted_iota(jnp.int32, (SGU_CHUNK, SGU_CHUNK), 1)
    bias = bs_ref[...]
    mixed = [[None] * SGU_GROUPS for _ in range(n_chunks)]
    for g in range(SGU_GROUPS):
        w = jnp.where(col <= row, ws_ref[g], 0.0).astype(BF16)
        rhs = jnp.concatenate(
            [vn[n * SGU_CHUNK:(n + 1) * SGU_CHUNK, g * gw:(g + 1) * gw] for n in range(n_chunks)], axis=1)
        res = _dot(w, rhs)
        for n in range(n_chunks):
            mixed[n][g] = res[:, n * gw:(n + 1) * gw]
    for n in range(n_chunks):
        m = jnp.concatenate(mixed[n], axis=1) + bias
        rows = slice(n * SGU_CHUNK, (n + 1) * SGU_CHUNK)
        out_ref[rows, :] = (u[rows, :] * m).astype(out_ref.dtype)


def _sgu_branch(h, wu, wv, lg, lb, ws, bs_full, *, tm=1024):
    t, d = h.shape
    c = SGU_WIDTH
    vec = pl.BlockSpec((1, c), lambda i: (0, 0))
    return pl.pallas_call(
        functools.partial(_sgu_body, tm=tm),
        grid=(t // tm,),
        in_specs=[
            pl.BlockSpec((tm, d), lambda i: (i, 0)),
            pl.BlockSpec((d, c), lambda i: (0, 0)),
            pl.BlockSpec((d, c), lambda i: (0, 0)),
            vec, vec,
            pl.BlockSpec((SGU_GROUPS, SGU_CHUNK, SGU_CHUNK), lambda i: (0, 0, 0)),
            pl.BlockSpec((SGU_CHUNK, c), lambda i: (0, 0)),
        ],
        out_specs=pl.BlockSpec((tm, c), lambda i: (i, 0)),
        out_shape=jax.ShapeDtypeStruct((t, c), BF16),
        compiler_params=_params(("parallel",)),
        name="sgu_branch",
    )(h, wu, wv, lg, lb, ws, bs_full)


def _qkv_body(h_ref, wqv_ref, wk_ref, rc_ref, rs1_ref, rs2_ref, ct_ref, st_ref, q_ref, k_ref, v_ref, km_ref):
    step = pl.program_id(1)

    @pl.when(step == 0)
    def _():
        km_ref[...] = jnp.zeros(km_ref.shape, F32)

    h = h_ref[...]
    qv_all = lax.dot_general(wqv_ref[...], h, NT_DIMS, preferred_element_type=F32)
    k_all = _dot(h, wk_ref[...])

    half = ROPE_DIM // 2
    tokens = MOBA_BLOCK
    scale = HEAD_DIM ** -0.5
    neg_inf = float("-inf")

    lane = lax.broadcasted_iota(jnp.int32, (tokens, LANES), 1)
    km_lane = lax.broadcasted_iota(jnp.int32, (MAX_BLOCKS, LANES), 1)
    blk_row = lax.broadcasted_iota(jnp.int32, (MAX_BLOCKS, tokens), 0)
    blk_row_f = blk_row.astype(F32)
    zeros_head = jnp.zeros((HEAD_DIM, tokens), F32)
    zeros_pad = jnp.zeros((AUG - 2 * HEAD_DIM - MAX_BLOCKS, 2 * tokens), F32)

    for sub in range(QKV_BLOCKS):
        blk = step * QKV_BLOCKS + sub
        rows = slice(sub * tokens, (sub + 1) * tokens)
        qv = qv_all[:, rows]
        k = k_all[rows, :]
        v_ref[0, sub] = qv[ATT_WIDTH:, :].astype(v_ref.dtype)
        rc, rs1, rs2 = rc_ref[rows, :], rs1_ref[rows, :], rs2_ref[rows, :]
        cos_t, sin_t = ct_ref[:, rows], st_ref[:, rows]
        onehot = jnp.where(lane == blk, 1.0, 0.0)

        def rope_t(x):
            x1, x2 = x[:half], x[half:ROPE_DIM]
            return jnp.concatenate([x1 * cos_t - x2 * sin_t, x2 * cos_t + x1 * sin_t, x[ROPE_DIM:]], axis=0)

        def route(gate):
            g = jnp.where(blk_row < blk, gate, neg_inf)
            bias = jnp.where(blk_row == blk, 0.0, MASK_VALUE)
            for _ in range(MOBA_TOPK):
                m = jnp.max(g, axis=0, keepdims=True)
                idx = jnp.min(jnp.where(g == m, blk_row_f, 2.0 * MAX_BLOCKS), axis=0, keepdims=True)
                pick = jnp.logical_and(blk_row_f == idx, m > neg_inf)
                bias = jnp.where(pick, 0.0, bias)
                g = jnp.where(pick, neg_inf, g)
            return bias

        for pair in range(ATT_HEADS // 2):
            even, odd = 2 * pair, 2 * pair + 1
            x = k[:, pair * LANES:(pair + 1) * LANES]
            ks = x * rc + pltpu.roll(x, LANES - half, 1) * rs1 + pltpu.roll(x, half, 1) * rs2
            k_ref[0, pair, rows, :] = jnp.concatenate([ks, onehot], axis=1).astype(k_ref.dtype)

            q_even = rope_t(qv[even * HEAD_DIM:(even + 1) * HEAD_DIM, :]) * scale
            q_odd = rope_t(qv[odd * HEAD_DIM:(odd + 1) * HEAD_DIM, :]) * scale
            q_pair = jnp.concatenate([q_even, q_odd], axis=0)

            km = km_ref[pair]
            gate_even = jnp.dot(jnp.where(km_lane < HEAD_DIM, km, 0.0), q_pair,
                                precision=lax.Precision.HIGHEST, preferred_element_type=F32)
            gate_odd = jnp.dot(jnp.where(km_lane >= HEAD_DIM, km, 0.0), q_pair,
                               precision=lax.Precision.HIGHEST, preferred_element_type=F32)
            q_ref[0, pair, sub] = jnp.concatenate([
                jnp.concatenate([q_even, zeros_head], axis=1),
                jnp.concatenate([zeros_head, q_odd], axis=1),
                jnp.concatenate([route(gate_even), route(gate_odd)], axis=1),
                zeros_pad], axis=0).astype(q_ref.dtype)

            km_ref[pair, pl.ds(blk, 1), :] = jnp.sum(ks, axis=0, keepdims=True) * (1.0 / MOBA_BLOCK)


def _qkv_route(h, wqv_t, wk, rc, rs1, rs2, cos_t, sin_t, *, batch, seq):
    t, d = h.shape
    blk_tokens = MOBA_BLOCK
    tm = QKV_BLOCKS * blk_tokens
    nb = seq // blk_tokens
    ns = seq // tm
    pairs = ATT_HEADS // 2
    r_spec = pl.BlockSpec((tm, LANES), lambda b, i: (i, 0))
    t_spec = pl.BlockSpec((ROPE_DIM // 2, tm), lambda b, i: (0, i))
    return pl.pallas_call(
        _qkv_body,
        grid=(batch, ns),
        in_specs=[pl.BlockSpec((tm, d), lambda b, i: (b * ns + i, 0)),
                  pl.BlockSpec((2 * ATT_WIDTH, d), lambda b, i: (0, 0)),
                  pl.BlockSpec((d, ATT_WIDTH), lambda b, i: (0, 0)),
                  r_spec, r_spec, r_spec, t_spec, t_spec],
        out_specs=(pl.BlockSpec((1, pairs, QKV_BLOCKS, AUG, 2 * blk_tokens), lambda b, i: (b, 0, i, 0, 0)),
                   pl.BlockSpec((1, pairs, tm, AUG), lambda b, i: (b, 0, i, 0)),
                   pl.BlockSpec((1, QKV_BLOCKS, ATT_WIDTH, blk_tokens), lambda b, i: (b, i, 0, 0))),
        out_shape=(jax.ShapeDtypeStruct((batch, pairs, nb, AUG, 2 * blk_tokens), BF16),
                   jax.ShapeDtypeStruct((batch, pairs, seq, AUG), BF16),
                   jax.ShapeDtypeStruct((batch, nb, ATT_WIDTH, blk_tokens), BF16)),
        scratch_shapes=[pltpu.VMEM((ATT_HEADS // 2, MAX_BLOCKS, LANES), F32)],
        compiler_params=_params(("arbitrary", "arbitrary")),
        name="qkv_route",
    )(h, wqv_t, wk, rc, rs1, rs2, cos_t, sin_t)


def _attn_body(q_ref, k_ref, v_ref, o_ref, m_ref, l_ref, acc_ref, sa_ref, sb_ref, ma_ref, mb_ref):
    blk = pl.program_id(2)
    tq = MOBA_BLOCK
    key = lax.broadcasted_iota(jnp.int32, (tq, 2 * tq), 0)
    lane = lax.broadcasted_iota(jnp.int32, (tq, 2 * tq), 1)
    qry = jnp.where(lane < tq, lane, lane - tq)
    causal = jnp.where(key <= qry, 0.0, MASK_VALUE)

    def scores(pp, first_blk, n_blk):
        start = pl.multiple_of(first_blk * tq, tq)
        return _dot(k_ref[0, pp, pl.ds(start, n_blk * tq), :], q_ref[0, pp, 0])

    def update(pp, s, first_blk, n_blk, m_cur=None):
        v_t = jnp.concatenate([v_ref[0, first_blk + a, pp * LANES:(pp + 1) * LANES, :]
                               for a in range(n_blk)], axis=1)
        if m_cur is None:
            m_cur = jnp.max(s, axis=0, keepdims=True)
        m_old = m_ref[pp]
        m_new = jnp.maximum(m_old, m_cur)
        alpha = jnp.exp(m_old - m_new)
        p = jnp.exp(s - m_new)
        l_ref[pp] = alpha * l_ref[pp] + jnp.sum(p, axis=0, keepdims=True)
        m_ref[pp] = m_new
        acc_ref[pp] = alpha * acc_ref[pp] + _dot(v_t, p.astype(BF16))

    for pp in range(ATT_PAIRS):
        m_ref[pp] = jnp.full((1, 2 * tq), MASK_VALUE, F32)
        l_ref[pp] = jnp.zeros((1, 2 * tq), F32)
        acc_ref[pp] = jnp.zeros((LANES, 2 * tq), F32)

    n_full = blk // KEY_STEP
    n_rem = blk - n_full * KEY_STEP

    def fill(bufs, grp, with_max=True):
        buf, mbuf = bufs
        ss = [scores(pp, grp * KEY_STEP, KEY_STEP) for pp in range(ATT_PAIRS)]
        for pp in range(ATT_PAIRS):
            buf[pp] = ss[pp]
            if with_max:
                mbuf[pp] = jnp.max(ss[pp], axis=0, keepdims=True)

    def drain(bufs, grp):
        buf, mbuf = bufs
        for pp in range(ATT_PAIRS):
            update(pp, buf[pp], grp * KEY_STEP, KEY_STEP, mbuf[pp])

    def drain_head(bufs, rem):
        buf, _ = bufs
        mask = causal if rem == 0 else jnp.concatenate(
            [jnp.zeros((rem * tq, 2 * tq), F32), causal], axis=0)
        for pp in range(ATT_PAIRS):
            update(pp, buf[pp, 0:(rem + 1) * tq, :] + mask, blk - rem, rem + 1)

    sa_ref, sb_ref = (sa_ref, ma_ref), (sb_ref, mb_ref)

    @pl.when(n_full > 0)
    def _():
        fill(sa_ref, 0)

    def two_groups(t, carry):
        grp = 2 * t
        fill(sb_ref, grp + 1)
        drain(sa_ref, grp)
        fill(sa_ref, grp + 2)
        drain(sb_ref, grp + 1)
        return carry

    n_trips = (n_full - 1) // 2
    lax.fori_loop(0, n_trips, two_groups, 0)
    grp_left = 2 * n_trips
    n_left = n_full - grp_left
    one_left = jnp.logical_and(n_full > 0, n_left == 1)

    @pl.when(one_left)
    def _():
        fill(sb_ref, n_full, with_max=False)
        drain(sa_ref, grp_left)

    @pl.when(jnp.logical_and(n_full > 0, n_left == 2))
    def _():
        fill(sb_ref, grp_left + 1)
        drain(sa_ref, grp_left)
        fill(sa_ref, n_full, with_max=False)
        drain(sb_ref, grp_left + 1)

    @pl.when(n_full == 0)
    def _():
        fill(sa_ref, 0, with_max=False)

    for rem in range(KEY_STEP):
        for bufs, in_b in ((sa_ref, False), (sb_ref, True)):
            @pl.when(jnp.logical_and(n_rem == rem, one_left if in_b else jnp.logical_not(one_left)))
            def _():
                drain_head(bufs, rem)

    for pp in range(ATT_PAIRS):
        inv_l = 1.0 / l_ref[pp]
        o = acc_ref[pp] * inv_l
        o_t = jnp.concatenate([o[:HEAD_DIM, :tq], o[HEAD_DIM:, tq:]], axis=0)
        o_ref[0, :, pp * LANES:(pp + 1) * LANES] = o_t.T.astype(o_ref.dtype)


def _moba_attention(q_aug, k_aug, v_t, *, batch, seq):
    tq = MOBA_BLOCK
    nb = seq // tq
    gw = ATT_PAIRS * LANES
    out = pl.pallas_call(
        _attn_body,
        grid=(batch, ATT_HEADS // (2 * ATT_PAIRS), nb),
        in_specs=[
            pl.BlockSpec((1, ATT_PAIRS, 1, AUG, 2 * tq), lambda b, g, i: (b, g, i, 0, 0)),
            pl.BlockSpec((1, ATT_PAIRS, seq, AUG), lambda b, g, i: (b, g, 0, 0), pipeline_mode=pl.Buffered(1)),
            pl.BlockSpec((1, nb, gw, tq), lambda b, g, i: (b, 0, g, 0), pipeline_mode=pl.Buffered(1)),
        ],
        out_specs=pl.BlockSpec((1, tq, gw), lambda b, g, i: (b, i, g)),
        out_shape=jax.ShapeDtypeStruct((batch, seq, ATT_WIDTH), BF16),
        scratch_shapes=[pltpu.VMEM((ATT_PAIRS, 1, 2 * tq), F32), pltpu.VMEM((ATT_PAIRS, 1, 2 * tq), F32),
                        pltpu.VMEM((ATT_PAIRS, LANES, 2 * tq), F32)]
        + [pltpu.VMEM((ATT_PAIRS, KEY_STEP * tq, 2 * tq), F32)] * 2
        + [pltpu.VMEM((ATT_PAIRS, 1, 2 * tq), F32)] * 2,
        compiler_params=_params(("parallel", "parallel", "arbitrary")),
        name="moba_attention",
    )(q_aug, k_aug, v_t)
    return out.reshape(batch * seq, ATT_WIDTH)


def _merge_body(x_ref, h_ref, ya_ref, yb_ref, yc_ref, wgate_ref, gb_ref, wbr_ref, wout_ref, out_ref):
    h = h_ref[...]
    d = x_ref.shape[1]
    merged = None
    for n, y_ref in enumerate((ya_ref, yb_ref, yc_ref)):
        gate = _sigmoid(_dot(h, wgate_ref[:, n * d:(n + 1) * d]) + gb_ref[:, n * d:(n + 1) * d])
        term = gate * _dot(y_ref[...], wbr_ref[n])
        merged = term if merged is None else merged + term
    out_ref[...] = x_ref[...] + _dot(merged.astype(BF16), wout_ref[...])


def _merge(x, h, ya, yb, yc, wgate, gb, wbr, wout, *, tm=512):
    t, d = x.shape
    c = CONV_WIDTH
    row_d = pl.BlockSpec((tm, d), lambda i: (i, 0))
    row_c = pl.BlockSpec((tm, c), lambda i: (i, 0))
    return pl.pallas_call(
        _merge_body,
        grid=(t // tm,),
        in_specs=[
            row_d, row_d, row_c, row_c, row_c,
            pl.BlockSpec((d, N_BRANCH * d), lambda i: (0, 0)),
            pl.BlockSpec((1, N_BRANCH * d), lambda i: (0, 0)),
            pl.BlockSpec((N_BRANCH, c, d), lambda i: (0, 0, 0)),
            pl.BlockSpec((d, d), lambda i: (0, 0)),
        ],
        out_specs=row_d,
        out_shape=jax.ShapeDtypeStruct((t, d), F32),
        compiler_params=_params(("parallel",)),
        name="merge",
    )(x, h, ya, yb, yc, wgate, gb, wbr, wout)


def _rope_tables(seq):
    pos = jnp.arange(seq, dtype=F32)
    inv_freq = ROPE_THETA ** (-jnp.arange(0, ROPE_DIM, 2, dtype=F32) / ROPE_DIM)
    ang = pos[:, None] * inv_freq[None, :]
    cos, sin = jnp.cos(ang), jnp.sin(ang)
    half = ROPE_DIM // 2
    ones = jnp.ones((seq, HEAD_DIM - ROPE_DIM), F32)
    zeros = jnp.zeros((seq, HEAD_DIM - ROPE_DIM), F32)
    zh = jnp.zeros((seq, half), F32)
    rc = jnp.concatenate([cos, cos, ones], axis=1)
    rs1 = jnp.concatenate([-sin, zh, zeros], axis=1)
    rs2 = jnp.concatenate([zh, sin, zeros], axis=1)
    reps = LANES // HEAD_DIM
    return tuple(jnp.tile(a, (1, reps)) for a in (rc, rs1, rs2)) + (cos.T, sin.T)


def kernel(x, ffn1_norm, ffn1_wi, ffn1_wo, mix_norm, w_in, conv_w, conv_b, conv_ln_g, conv_ln_b,
           sgu_ln_g, sgu_ln_b, sgu_w, sgu_b, w_branch, gate_b, w_out, ffn2_norm, ffn2_wi, ffn2_wo,
           final_norm):
    batch, seq, d = x.shape
    depth = ffn1_norm.shape[0]
    assert seq % MOBA_BLOCK == 0 and seq % 512 == 0 and d % LANES == 0
    assert seq // MOBA_BLOCK <= MAX_BLOCKS and (seq // MOBA_BLOCK) % KEY_STEP == 0
    xt = x.reshape(batch * seq, d)
    rc, rs1, rs2, cos_t, sin_t = _rope_tables(seq)
    gw = SGU_WIDTH // SGU_GROUPS

    o_sgu = 2 * CONV_WIDTH
    o_q = o_sgu + 2 * SGU_WIDTH
    o_k, o_v, o_gate = o_q + ATT_WIDTH, o_q + 2 * ATT_WIDTH, o_q + 3 * ATT_WIDTH

    def row(a):
        return a.reshape(1, -1)

    for l in range(depth):
        wl = w_in[l]

        def cols(a, b):
            return wl[:, a:b].astype(BF16)

        xt, h = _ffn(xt, row(ffn1_norm[l]), ffn1_wi[l].astype(BF16), ffn1_wo[l].astype(BF16),
                     row(mix_norm[l]), post="both")

        cw8 = jnp.broadcast_to(conv_w[l][:, None, :], (CONV_K, SUBLANES, CONV_WIDTH))
        ya = _conv_branch(h, cols(0, CONV_WIDTH), cols(CONV_WIDTH, o_sgu), cw8, row(conv_b[l]),
                          row(conv_ln_g[l]), row(conv_ln_b[l]), seq=seq)

        bs_full = jnp.repeat(sgu_b[l].T, gw, axis=1)
        yb = _sgu_branch(h, cols(o_sgu, o_sgu + SGU_WIDTH), cols(o_sgu + SGU_WIDTH, o_q),
                         row(sgu_ln_g[l]), row(sgu_ln_b[l]), sgu_w[l], bs_full)

        wqv_t = jnp.concatenate([wl[:, o_q:o_k], wl[:, o_v:o_gate]], axis=1).T.astype(BF16)
        q_aug_t, k_aug, v_t = _qkv_route(h, wqv_t, cols(o_k, o_v), rc, rs1, rs2, cos_t, sin_t,
                                         batch=batch, seq=seq)
        yc = _moba_attention(q_aug_t, k_aug, v_t, batch=batch, seq=seq)

        xt = _merge(xt, h, ya, yb, yc, cols(o_gate, wl.shape[1]), row(gate_b[l]),
                    w_branch[l].astype(BF16), w_out[l].astype(BF16))

        last = l == depth - 1
        xt = _ffn(xt, row(ffn2_norm[l]), ffn2_wi[l].astype(BF16), ffn2_wo[l].astype(BF16),
                  row(final_norm), post="norm_only" if last else "none")
    return xt.reshape(batch, seq, d)
```

```python
import functools

import numpy as np
import jax
import jax.numpy as jnp
from jax import lax
from jax.experimental import pallas as pl
from jax.experimental.pallas import tpu as pltpu

D_FF = 2816
CONV_WIDTH = 512
CONV_K = 31
SGU_WIDTH = 512
SGU_GROUPS = 4
SGU_CHUNK = 128
ATT_HEADS = 8
HEAD_DIM = 64
ATT_WIDTH = ATT_HEADS * HEAD_DIM
MOBA_BLOCK = 256
MOBA_TOPK = 3
ROPE_THETA = 500000.0
ROPE_DIM = HEAD_DIM // 4
N_BRANCH = 3
EPS = 1e-6

LANES = 128
SUBLANES = 8
VMEM_LIMIT = 56 * 1024 * 1024
MASK_VALUE = -1e30
CONV_HALO = 32
CONV_ROWS = 64
QKV_BLOCKS = 4
ATT_PAIRS = 4
KEY_STEP = 4
MAX_BLOCKS = 32
AUG = 2 * LANES

BF16 = jnp.bfloat16
F32 = jnp.float32
NT_DIMS = (((1,), (1,)), ((), ()))


def _dot(a, b):
    return jnp.dot(a, b, preferred_element_type=F32)


def _sigmoid(x):
    return 1.0 / (1.0 + jnp.exp(-x))


def _rms(x, g):
    return x * lax.rsqrt(jnp.mean(x * x, axis=-1, keepdims=True) + EPS) * g


def _params(sem, flags=None):
    return pltpu.CompilerParams(dimension_semantics=sem, vmem_limit_bytes=VMEM_LIMIT, flags=flags)


def _ffn_body(x_ref, g_ref, wg_ref, wu_ref, wo_ref, g2_ref, *refs, post, single_step):
    if post == "both":
        out_ref, nxt_ref = refs[:2]
    else:
        out_ref = refs[0]

    def finish(acc, rows=slice(None)):
        y = x_ref[rows, :] + 0.5 * acc
        if post == "both":
            out_ref[rows, :] = y
            nxt_ref[rows, :] = _rms(y, g2_ref[...]).astype(nxt_ref.dtype)
        elif post == "norm_only":
            out_ref[rows, :] = _rms(y, g2_ref[...])
        else:
            out_ref[rows, :] = y

    def partial_out(xn):
        gate = _dot(xn, wg_ref[...])
        up = _dot(xn, wu_ref[...])
        act = (gate * _sigmoid(gate) * up).astype(BF16)
        return _dot(act, wo_ref[...])

    if single_step:
        half = x_ref.shape[0] // 2
        for c in range(2):
            rows = slice(c * half, (c + 1) * half)
            finish(partial_out(_rms(x_ref[rows, :], g_ref[...]).astype(BF16)), rows)
        return

    xn_ref, acc_ref = refs[-2:]
    j = pl.program_id(1)

    @pl.when(j == 0)
    def _():
        xn_ref[...] = _rms(x_ref[...], g_ref[...]).astype(BF16)

    part = partial_out(xn_ref[...])

    @pl.when(j == 0)
    def _():
        acc_ref[...] = part

    @pl.when(j > 0)
    def _():
        acc_ref[...] += part

    @pl.when(j == pl.num_programs(1) - 1)
    def _():
        finish(acc_ref[...])


def _ffn(x, g, wi, wo, g2, *, post, tm=512, tf=D_FF):
    t, d = x.shape
    nf = D_FF // tf
    single = nf == 1
    w_mode = dict(pipeline_mode=pl.Buffered(1)) if single else {}
    row = pl.BlockSpec((tm, d), lambda i, j: (i, 0))
    vec = pl.BlockSpec((1, d), lambda i, j: (0, 0))
    in_specs = [
        row, vec,
        pl.BlockSpec((d, tf), lambda i, j: (0, j), **w_mode),
        pl.BlockSpec((d, tf), lambda i, j: (0, j + nf), **w_mode),
        pl.BlockSpec((tf, d), lambda i, j: (j, 0), **w_mode),
        vec,
    ]
    if post == "both":
        out_shape = (jax.ShapeDtypeStruct((t, d), F32), jax.ShapeDtypeStruct((t, d), BF16))
        out_specs = (row, row)
    else:
        out_shape = jax.ShapeDtypeStruct((t, d), F32)
        out_specs = row
    return pl.pallas_call(
        functools.partial(_ffn_body, post=post, single_step=single),
        grid=(t // tm, nf),
        in_specs=in_specs,
        out_specs=out_specs,
        out_shape=out_shape,
        scratch_shapes=[] if single else [pltpu.VMEM((tm, d), BF16), pltpu.VMEM((tm, d), F32)],
        compiler_params=_params(("parallel", "arbitrary")),
        name="ffn",
    )(x, g, wi, wi, wo, g2)


def _conv_body(h_ref, wa_ref, wg_ref, cw_ref, cb_ref, lg_ref, lb_ref, out_ref, buf_ref, conv_ref, *, tm,
               tiles_per_seq):
    t = pl.program_id(0)

    @pl.when(t % tiles_per_seq == 0)
    def _():
        buf_ref[...] = jnp.zeros(buf_ref.shape, F32)

    h = h_ref[...]
    glu = _dot(h, wa_ref[...]) * _sigmoid(_dot(h, wg_ref[...]))
    for r in range(SUBLANES):
        buf_ref[r, CONV_HALO - r:CONV_HALO - r + tm, :] = glu

    first_tap = CONV_HALO - (CONV_K - 1)

    def step(i, carry):
        base = pl.multiple_of(i * CONV_ROWS, CONV_ROWS)
        y = None
        for k in range(CONV_K):
            o = first_tap + k
            win = buf_ref[o % SUBLANES, pl.ds(base + (o - o % SUBLANES), CONV_ROWS), :]
            term = win.reshape(CONV_ROWS // SUBLANES, SUBLANES, CONV_WIDTH) * cw_ref[k]
            y = term if y is None else y + term
        conv_ref[pl.ds(base, CONV_ROWS), :] = y.reshape(CONV_ROWS, CONV_WIDTH)
        return carry

    lax.fori_loop(0, tm // CONV_ROWS, step, 0)
    for r in range(SUBLANES):
        buf_ref[r, 0:CONV_HALO, :] = buf_ref[r, tm:tm + CONV_HALO, :]

    y = conv_ref[...] + cb_ref[...]
    mu = jnp.mean(y, axis=-1, keepdims=True)
    yc = y - mu
    var = jnp.mean(yc * yc, axis=-1, keepdims=True)
    yn = yc * lax.rsqrt(var + EPS) * lg_ref[...] + lb_ref[...]
    out_ref[...] = (yn * _sigmoid(yn)).astype(out_ref.dtype)


def _conv_branch(h, wa, wg, cw, cb, lg, lb, *, seq, tm=1024):
    t, d = h.shape
    c = CONV_WIDTH
    vec = pl.BlockSpec((1, c), lambda i: (0, 0))
    return pl.pallas_call(
        functools.partial(_conv_body, tm=tm, tiles_per_seq=seq // tm),
        grid=(t // tm,),
        in_specs=[
            pl.BlockSpec((tm, d), lambda i: (i, 0)),
            pl.BlockSpec((d, c), lambda i: (0, 0)),
            pl.BlockSpec((d, c), lambda i: (0, 0)),
            pl.BlockSpec((CONV_K, SUBLANES, c), lambda i: (0, 0, 0)),
            vec, vec, vec,
        ],
        out_specs=pl.BlockSpec((tm, c), lambda i: (i, 0)),
        out_shape=jax.ShapeDtypeStruct((t, c), BF16),
        scratch_shapes=[pltpu.VMEM((SUBLANES, CONV_HALO + tm, c), F32), pltpu.VMEM((tm, c), F32)],
        compiler_params=_params(("arbitrary",)),
        name="conv_branch",
    )(h, wa, wg, cw, cb, lg, lb)


def _gelu_tanh(x):
    c = np.float32(np.sqrt(2.0 / np.pi))
    return x * (0.5 * (1.0 + jnp.tanh(c * (x + 0.044715 * (x * x * x)))))


def _sgu_body(h_ref, wu_ref, wv_ref, lg_ref, lb_ref, ws_ref, bs_ref, out_ref, *, tm):
    h = h_ref[...]
    u = _gelu_tanh(_dot(h, wu_ref[...]))
    v = _gelu_tanh(_dot(h, wv_ref[...]))
    mu = jnp.mean(v, axis=-1, keepdims=True)
    vc = v - mu
    var = jnp.mean(vc * vc, axis=-1, keepdims=True)
    vn = (vc * lax.rsqrt(var + EPS) * lg_ref[...] + lb_ref[...]).astype(BF16)

    n_chunks = tm // SGU_CHUNK
    gw = SGU_WIDTH // SGU_GROUPS
    row = lax.broadcasted_iota(jnp.int32, (SGU_CHUNK, SGU_CHUNK), 0)
    col = lax.broadcasted_iota(jnp.int32, (SGU_CHUNK, SGU_CHUNK), 1)
    bias = bs_ref[...]
    mixed = [[None] * SGU_GROUPS for _ in range(n_chunks)]
    for g in range(SGU_GROUPS):
        w = jnp.where(col <= row, ws_ref[g], 0.0).astype(BF16)
        rhs = jnp.concatenate(
            [vn[n * SGU_CHUNK:(n + 1) * SGU_CHUNK, g * gw:(g + 1) * gw] for n in range(n_chunks)], axis=1)
        res = _dot(w, rhs)
        for n in range(n_chunks):
            mixed[n][g] = res[:, n * gw:(n + 1) * gw]
    for n in range(n_chunks):
        m = jnp.concatenate(mixed[n], axis=1) + bias
        rows = slice(n * SGU_CHUNK, (n + 1) * SGU_CHUNK)
        out_ref[rows, :] = (u[rows, :] * m).astype(out_ref.dtype)


def _sgu_branch(h, wu, wv, lg, lb, ws, bs_full, *, tm=1024):
    t, d = h.shape
    c = SGU_WIDTH
    vec = pl.BlockSpec((1, c), lambda i: (0, 0))
    return pl.pallas_call(
        functools.partial(_sgu_body, tm=tm),
        grid=(t // tm,),
        in_specs=[
            pl.BlockSpec((tm, d), lambda i: (i, 0)),
            pl.BlockSpec((d, c), lambda i: (0, 0)),
            pl.BlockSpec((d, c), lambda i: (0, 0)),
            vec, vec,
            pl.BlockSpec((SGU_GROUPS, SGU_CHUNK, SGU_CHUNK), lambda i: (0, 0, 0)),
            pl.BlockSpec((SGU_CHUNK, c), lambda i: (0, 0)),
        ],
        out_specs=pl.BlockSpec((tm, c), lambda i: (i, 0)),
        out_shape=jax.ShapeDtypeStruct((t, c), BF16),
        compiler_params=_params(("parallel",)),
        name="sgu_branch",
    )(h, wu, wv, lg, lb, ws, bs_full)


def _qkv_body(h_ref, wqv_ref, wk_ref, rc_ref, rs1_ref, rs2_ref, ct_ref, st_ref, q_ref, k_ref, v_ref, km_ref):
    step = pl.program_id(1)

    @pl.when(step == 0)
    def _():
        km_ref[...] = jnp.zeros(km_ref.shape, F32)

    h = h_ref[...]
    qv_all = lax.dot_general(wqv_ref[...], h, NT_DIMS, preferred_element_type=F32)
    k_all = _dot(h, wk_ref[...])

    half = ROPE_DIM // 2
    tokens = MOBA_BLOCK
    scale = HEAD_DIM ** -0.5
    neg_inf = float("-inf")

    lane = lax.broadcasted_iota(jnp.int32, (tokens, LANES), 1)
    km_lane = lax.broadcasted_iota(jnp.int32, (MAX_BLOCKS, LANES), 1)
    blk_row = lax.broadcasted_iota(jnp.int32, (MAX_BLOCKS, tokens), 0)
    blk_row_f = blk_row.astype(F32)
    zeros_head = jnp.zeros((HEAD_DIM, tokens), F32)
    zeros_pad = jnp.zeros((AUG - 2 * HEAD_DIM - MAX_BLOCKS, 2 * tokens), F32)

    for sub in range(QKV_BLOCKS):
        blk = step * QKV_BLOCKS + sub
        rows = slice(sub * tokens, (sub + 1) * tokens)
        qv = qv_all[:, rows]
        k = k_all[rows, :]
        v_ref[0, sub] = qv[ATT_WIDTH:, :].astype(v_ref.dtype)
        rc, rs1, rs2 = rc_ref[rows, :], rs1_ref[rows, :], rs2_ref[rows, :]
        cos_t, sin_t = ct_ref[:, rows], st_ref[:, rows]
        onehot = jnp.where(lane == blk, 1.0, 0.0)

        def rope_t(x):
            x1, x2 = x[:half], x[half:ROPE_DIM]
            return jnp.concatenate([x1 * cos_t - x2 * sin_t, x2 * cos_t + x1 * sin_t, x[ROPE_DIM:]], axis=0)

        def route(gate):
            g = jnp.where(blk_row < blk, gate, neg_inf)
            bias = jnp.where(blk_row == blk, 0.0, MASK_VALUE)
            for _ in range(MOBA_TOPK):
                m = jnp.max(g, axis=0, keepdims=True)
                idx = jnp.min(jnp.where(g == m, blk_row_f, 2.0 * MAX_BLOCKS), axis=0, keepdims=True)
                pick = jnp.logical_and(blk_row_f == idx, m > neg_inf)
                bias = jnp.where(pick, 0.0, bias)
                g = jnp.where(pick, neg_inf, g)
            return bias

        for pair in range(ATT_HEADS // 2):
            even, odd = 2 * pair, 2 * pair + 1
            x = k[:, pair * LANES:(pair + 1) * LANES]
            ks = x * rc + pltpu.roll(x, LANES - half, 1) * rs1 + pltpu.roll(x, half, 1) * rs2
            k_ref[0, pair, rows, :] = jnp.concatenate([ks, onehot], axis=1).astype(k_ref.dtype)

            q_even = rope_t(qv[even * HEAD_DIM:(even + 1) * HEAD_DIM, :]) * scale
            q_odd = rope_t(qv[odd * HEAD_DIM:(odd + 1) * HEAD_DIM, :]) * scale
            q_pair = jnp.concatenate([q_even, q_odd], axis=0)

            km = km_ref[pair]
            gate_even = jnp.dot(jnp.where(km_lane < HEAD_DIM, km, 0.0), q_pair,
                                precision=lax.Precision.HIGHEST, preferred_element_type=F32)
            gate_odd = jnp.dot(jnp.where(km_lane >= HEAD_DIM, km, 0.0), q_pair,
                               precision=lax.Precision.HIGHEST, preferred_element_type=F32)
            q_ref[0, pair, sub] = jnp.concatenate([
                jnp.concatenate([q_even, zeros_head], axis=1),
                jnp.concatenate([zeros_head, q_odd], axis=1),
                jnp.concatenate([route(gate_even), route(gate_odd)], axis=1),
                zeros_pad], axis=0).astype(q_ref.dtype)

            km_ref[pair, pl.ds(blk, 1), :] = jnp.sum(ks, axis=0, keepdims=True) * (1.0 / MOBA_BLOCK)


def _qkv_route(h, wqv_t, wk, rc, rs1, rs2, cos_t, sin_t, *, batch, seq):
    t, d = h.shape
    blk_tokens = MOBA_BLOCK
    tm = QKV_BLOCKS * blk_tokens
    nb = seq // blk_tokens
    ns = seq // tm
    pairs = ATT_HEADS // 2
    r_spec = pl.BlockSpec((tm, LANES), lambda b, i: (i, 0))
    t_spec = pl.BlockSpec((ROPE_DIM // 2, tm), lambda b, i: (0, i))
    return pl.pallas_call(
        _qkv_body,
        grid=(batch, ns),
        in_specs=[pl.BlockSpec((tm, d), lambda b, i: (b * ns + i, 0)),
                  pl.BlockSpec((2 * ATT_WIDTH, d), lambda b, i: (0, 0)),
                  pl.BlockSpec((d, ATT_WIDTH), lambda b, i: (0, 0)),
                  r_spec, r_spec, r_spec, t_spec, t_spec],
        out_specs=(pl.BlockSpec((1, pairs, QKV_BLOCKS, AUG, 2 * blk_tokens), lambda b, i: (b, 0, i, 0, 0)),
                   pl.BlockSpec((1, pairs, tm, AUG), lambda b, i: (b, 0, i, 0)),
                   pl.BlockSpec((1, QKV_BLOCKS, ATT_WIDTH, blk_tokens), lambda b, i: (b, i, 0, 0))),
        out_shape=(jax.ShapeDtypeStruct((batch, pairs, nb, AUG, 2 * blk_tokens), BF16),
                   jax.ShapeDtypeStruct((batch, pairs, seq, AUG), BF16),
                   jax.ShapeDtypeStruct((batch, nb, ATT_WIDTH, blk_tokens), BF16)),
        scratch_shapes=[pltpu.VMEM((ATT_HEADS // 2, MAX_BLOCKS, LANES), F32)],
        compiler_params=_params(("arbitrary", "arbitrary")),
        name="qkv_route",
    )(h, wqv_t, wk, rc, rs1, rs2, cos_t, sin_t)


def _attn_body(q_ref, k_ref, v_ref, o_ref, m_ref, l_ref, acc_ref, sa_ref, sb_ref, ma_ref, mb_ref):
    blk = pl.program_id(2)
    tq = MOBA_BLOCK
    key = lax.broadcasted_iota(jnp.int32, (tq, 2 * tq), 0)
    lane = lax.broadcasted_iota(jnp.int32, (tq, 2 * tq), 1)
    qry = jnp.where(lane < tq, lane, lane - tq)
    causal = jnp.where(key <= qry, 0.0, MASK_VALUE)

    def scores(pp, first_blk, n_blk):
        start = pl.multiple_of(first_blk * tq, tq)
        return _dot(k_ref[0, pp, pl.ds(start, n_blk * tq), :], q_ref[0, pp, 0])

    def update(pp, s, first_blk, n_blk, m_cur=None):
        v_t = jnp.concatenate([v_ref[0, first_blk + a, pp * LANES:(pp + 1) * LANES, :]
                               for a in range(n_blk)], axis=1)
        if m_cur is None:
            m_cur = jnp.max(s, axis=0, keepdims=True)
        m_old = m_ref[pp]
        m_new = jnp.maximum(m_old, m_cur)
        alpha = jnp.exp(m_old - m_new)
        p = jnp.exp(s - m_new)
        l_ref[pp] = alpha * l_ref[pp] + jnp.sum(p, axis=0, keepdims=True)
        m_ref[pp] = m_new
        acc_ref[pp] = alpha * acc_ref[pp] + _dot(v_t, p.astype(BF16))

    for pp in range(ATT_PAIRS):
        m_ref[pp] = jnp.full((1, 2 * tq), MASK_VALUE, F32)
        l_ref[pp] = jnp.zeros((1, 2 * tq), F32)
        acc_ref[pp] = jnp.zeros((LANES, 2 * tq), F32)

    n_full = blk // KEY_STEP
    n_rem = blk - n_full * KEY_STEP

    def fill(bufs, grp, with_max=True):
        buf, mbuf = bufs
        ss = [scores(pp, grp * KEY_STEP, KEY_STEP) for pp in range(ATT_PAIRS)]
        for pp in range(ATT_PAIRS):
            buf[pp] = ss[pp]
            if with_max:
                mbuf[pp] = jnp.max(ss[pp], axis=0, keepdims=True)

    def drain(bufs, grp):
        buf, mbuf = bufs
        for pp in range(ATT_PAIRS):
            update(pp, buf[pp], grp * KEY_STEP, KEY_STEP, mbuf[pp])

    def drain_head(bufs, rem):
        buf, _ = bufs
        mask = causal if rem == 0 else jnp.concatenate(
            [jnp.zeros((rem * tq, 2 * tq), F32), causal], axis=0)
        for pp in range(ATT_PAIRS):
            update(pp, buf[pp, 0:(rem + 1) * tq, :] + mask, blk - rem, rem + 1)

    sa_ref, sb_ref = (sa_ref, ma_ref), (sb_ref, mb_ref)

    @pl.when(n_full > 0)
    def _():
        fill(sa_ref, 0)

    def two_groups(t, carry):
        grp = 2 * t
        fill(sb_ref, grp + 1)
        drain(sa_ref, grp)
        fill(sa_ref, grp + 2)
        drain(sb_ref, grp + 1)
        return carry

    n_trips = (n_full - 1) // 2
    lax.fori_loop(0, n_trips, two_groups, 0)
    grp_left = 2 * n_trips
    n_left = n_full - grp_left
    one_left = jnp.logical_and(n_full > 0, n_left == 1)

    @pl.when(one_left)
    def _():
        fill(sb_ref, n_full, with_max=False)
        drain(sa_ref, grp_left)

    @pl.when(jnp.logical_and(n_full > 0, n_left == 2))
    def _():
        fill(sb_ref, grp_left + 1)
        drain(sa_ref, grp_left)
        fill(sa_ref, n_full, with_max=False)
        drain(sb_ref, grp_left + 1)

    @pl.when(n_full == 0)
    def _():
        fill(sa_ref, 0, with_max=False)

    for rem in range(KEY_STEP):
        for bufs, in_b in ((sa_ref, False), (sb_ref, True)):
            @pl.when(jnp.logical_and(n_rem == rem, one_left if in_b else jnp.logical_not(one_left)))
            def _():
                drain_head(bufs, rem)

    for pp in range(ATT_PAIRS):
        inv_l = 1.0 / l_ref[pp]
        o = acc_ref[pp] * inv_l
        o_t = jnp.concatenate([o[:HEAD_DIM, :tq], o[HEAD_DIM:, tq:]], axis=0)
        o_ref[0, :, pp * LANES:(pp + 1) * LANES] = o_t.T.astype(o_ref.dtype)


def _moba_attention(q_aug, k_aug, v_t, *, batch, seq):
    tq = MOBA_BLOCK
    nb = seq // tq
    gw = ATT_PAIRS * LANES
    out = pl.pallas_call(
        _attn_body,
        grid=(batch, ATT_HEADS // (2 * ATT_PAIRS), nb),
        in_specs=[
            pl.BlockSpec((1, ATT_PAIRS, 1, AUG, 2 * tq), lambda b, g, i: (b, g, i, 0, 0)),
            pl.BlockSpec((1, ATT_PAIRS, seq, AUG), lambda b, g, i: (b, g, 0, 0), pipeline_mode=pl.Buffered(1)),
            pl.BlockSpec((1, nb, gw, tq), lambda b, g, i: (b, 0, g, 0), pipeline_mode=pl.Buffered(1)),
        ],
        out_specs=pl.BlockSpec((1, tq, gw), lambda b, g, i: (b, i, g)),
        out_shape=jax.ShapeDtypeStruct((batch, seq, ATT_WIDTH), BF16),
        scratch_shapes=[pltpu.VMEM((ATT_PAIRS, 1, 2 * tq), F32), pltpu.VMEM((ATT_PAIRS, 1, 2 * tq), F32),
                        pltpu.VMEM((ATT_PAIRS, LANES, 2 * tq), F32)]
        + [pltpu.VMEM((ATT_PAIRS, KEY_STEP * tq, 2 * tq), F32)] * 2
        + [pltpu.VMEM((ATT_PAIRS, 1, 2 * tq), F32)] * 2,
        compiler_params=_params(("parallel", "parallel", "arbitrary")),
        name="moba_attention",
    )(q_aug, k_aug, v_t)
    return out.reshape(batch * seq, ATT_WIDTH)


def _merge_body(x_ref, h_ref, ya_ref, yb_ref, yc_ref, wgate_ref, gb_ref, wbr_ref, wout_ref, out_ref):
    d = x_ref.shape[1]
    half = x_ref.shape[0] // 2
    for c in range(2):
        rows = slice(c * half, (c + 1) * half)
        h = h_ref[rows, :]
        merged = None
        for n, y_ref in enumerate((ya_ref, yb_ref, yc_ref)):
            gate = _sigmoid(_dot(h, wgate_ref[:, n * d:(n + 1) * d]) + gb_ref[:, n * d:(n + 1) * d])
            term = gate * _dot(y_ref[rows, :], wbr_ref[n])
            merged = term if merged is None else merged + term
        out_ref[rows, :] = x_ref[rows, :] + _dot(merged.astype(BF16), wout_ref[...])


def _merge(x, h, ya, yb, yc, wgate, gb, wbr, wout, *, tm=1024):
    t, d = x.shape
    c = CONV_WIDTH
    row_d = pl.BlockSpec((tm, d), lambda i: (i, 0))
    row_c = pl.BlockSpec((tm, c), lambda i: (i, 0))
    once = dict(pipeline_mode=pl.Buffered(1))
    return pl.pallas_call(
        _merge_body,
        grid=(t // tm,),
        in_specs=[
            row_d, row_d, row_c, row_c, row_c,
            pl.BlockSpec((d, N_BRANCH * d), lambda i: (0, 0), **once),
            pl.BlockSpec((1, N_BRANCH * d), lambda i: (0, 0)),
            pl.BlockSpec((N_BRANCH, c, d), lambda i: (0, 0, 0), **once),
            pl.BlockSpec((d, d), lambda i: (0, 0), **once),
        ],
        out_specs=row_d,
        out_shape=jax.ShapeDtypeStruct((t, d), F32),
        compiler_params=_params(("parallel",)),
        name="merge",
    )(x, h, ya, yb, yc, wgate, gb, wbr, wout)


def _rope_tables(seq):
    pos = jnp.arange(seq, dtype=F32)
    inv_freq = ROPE_THETA ** (-jnp.arange(0, ROPE_DIM, 2, dtype=F32) / ROPE_DIM)
    ang = pos[:, None] * inv_freq[None, :]
    cos, sin = jnp.cos(ang), jnp.sin(ang)
    half = ROPE_DIM // 2
    ones = jnp.ones((seq, HEAD_DIM - ROPE_DIM), F32)
    zeros = jnp.zeros((seq, HEAD_DIM - ROPE_DIM), F32)
    zh = jnp.zeros((seq, half), F32)
    rc = jnp.concatenate([cos, cos, ones], axis=1)
    rs1 = jnp.concatenate([-sin, zh, zeros], axis=1)
    rs2 = jnp.concatenate([zh, sin, zeros], axis=1)
    reps = LANES // HEAD_DIM
    return tuple(jnp.tile(a, (1, reps)) for a in (rc, rs1, rs2)) + (cos.T, sin.T)


def kernel(x, ffn1_norm, ffn1_wi, ffn1_wo, mix_norm, w_in, conv_w, conv_b, conv_ln_g, conv_ln_b,
           sgu_ln_g, sgu_ln_b, sgu_w, sgu_b, w_branch, gate_b, w_out, ffn2_norm, ffn2_wi, ffn2_wo,
           final_norm):
    batch, seq, d = x.shape
    depth = ffn1_norm.shape[0]
    assert seq % MOBA_BLOCK == 0 and seq % 512 == 0 and d % LANES == 0
    assert seq // MOBA_BLOCK <= MAX_BLOCKS and (seq // MOBA_BLOCK) % KEY_STEP == 0
    xt = x.reshape(batch * seq, d)
    rc, rs1, rs2, cos_t, sin_t = _rope_tables(seq)
    gw = SGU_WIDTH // SGU_GROUPS

    o_sgu = 2 * CONV_WIDTH
    o_q = o_sgu + 2 * SGU_WIDTH
    o_k, o_v, o_gate = o_q + ATT_WIDTH, o_q + 2 * ATT_WIDTH, o_q + 3 * ATT_WIDTH

    def row(a):
        return a.reshape(1, -1)

    for l in range(depth):
        wl = w_in[l]

        def cols(a, b):
            return wl[:, a:b].astype(BF16)

        xt, h = _ffn(xt, row(ffn1_norm[l]), ffn1_wi[l].astype(BF16), ffn1_wo[l].astype(BF16),
                     row(mix_norm[l]), post="both")

        cw8 = jnp.broadcast_to(conv_w[l][:, None, :], (CONV_K, SUBLANES, CONV_WIDTH))
        ya = _conv_branch(h, cols(0, CONV_WIDTH), cols(CONV_WIDTH, o_sgu), cw8, row(conv_b[l]),
                          row(conv_ln_g[l]), row(conv_ln_b[l]), seq=seq)

        bs_full = jnp.repeat(sgu_b[l].T, gw, axis=1)
        yb = _sgu_branch(h, cols(o_sgu, o_sgu + SGU_WIDTH), cols(o_sgu + SGU_WIDTH, o_q),
                         row(sgu_ln_g[l]), row(sgu_ln_b[l]), sgu_w[l], bs_full)

        wqv_t = jnp.concatenate([wl[:, o_q:o_k], wl[:, o_v:o_gate]], axis=1).T.astype(BF16)
        q_aug_t, k_aug, v_t = _qkv_route(h, wqv_t, cols(o_k, o_v), rc, rs1, rs2, cos_t, sin_t,
                                         batch=batch, seq=seq)
        yc = _moba_attention(q_aug_t, k_aug, v_t, batch=batch, seq=seq)

        xt = _merge(xt, h, ya, yb, yc, cols(o_gate, wl.shape[1]), row(gate_b[l]),
                    w_branch[l].astype(BF16), w_out[l].astype(BF16))

        last = l == depth - 1
        xt = _ffn(xt, row(ffn2_norm[l]), ffn2_wi[l].astype(BF16), ffn2_wo[l].astype(BF16),
                  row(final_norm), post="norm_only" if last else "none")
    return xt.reshape(batch, seq, d)
```
